```python
import math, functools
import jax, jax.numpy as jnp
from jax import lax
import numpy as np

D_MODEL = 4096
BATCH = 8
SEQ = 4096
DEPTH = 1

CHUNK = 64
LEFT_CHUNKS = 8
BAND = LEFT_CHUNKS + 1
ATT_WIDTH = D_MODEL // 2
CONV_WIDTH = D_MODEL - ATT_WIDTH
N_HEADS = 16
HEAD_DIM = ATT_WIDTH // N_HEADS
MAX_REL = 256
CONV_K = 31
PLE_DIM = 256
EPS = 1e-6
NEG_INF = -1e30
IN_COLS = 4 * ATT_WIDTH + 3 * CONV_WIDTH

kernel_name = "hymba_chunked_attn_conformer_conv_block"


def rms_norm(x, g):
    xf = x.astype(jnp.float32)
    y = xf * lax.rsqrt(jnp.mean(xf * xf, axis=-1, keepdims=True) + EPS)
    return (y * g.astype(jnp.float32)).astype(x.dtype)


def layer_norm(x, g, b):
    xf = x.astype(jnp.float32)
    mu = jnp.mean(xf, axis=-1, keepdims=True)
    xc = xf - mu
    var = jnp.mean(xc * xc, axis=-1, keepdims=True)
    y = xc * lax.rsqrt(var + EPS) * g.astype(jnp.float32) + b.astype(jnp.float32)
    return y.astype(x.dtype)


def rel_bias(table):
    i = np.arange(CHUNK)[:, None, None]
    j = np.arange(BAND)[None, :, None]
    k = np.arange(CHUNK)[None, None, :]
    dist = (LEFT_CHUNKS - j) * CHUNK + i - k
    idx = np.clip(dist, -MAX_REL, MAX_REL) + MAX_REL
    return table[:, idx].astype(jnp.float32)


def chunked_attention(q, k, v, table):
    B, T, H, Dh = q.shape
    NC = T // CHUNK
    qc = (q * (Dh ** -0.5)).reshape(B, NC, CHUNK, H, Dh)
    pad = ((0, 0), (LEFT_CHUNKS * CHUNK, 0), (0, 0), (0, 0))
    kc = jnp.pad(k, pad).reshape(B, NC + LEFT_CHUNKS, CHUNK, H, Dh)
    vc = jnp.pad(v, pad).reshape(B, NC + LEFT_CHUNKS, CHUNK, H, Dh)
    s = jnp.stack(
        [jnp.einsum('bcqhd,bckhd->bhcqk', qc, kc[:, j:j + NC],
                    preferred_element_type=jnp.float32) for j in range(BAND)],
        axis=4)
    s = s + rel_bias(table)[None, :, None]
    valid = (np.arange(NC)[:, None] - LEFT_CHUNKS + np.arange(BAND)[None, :]) >= 0
    s = jnp.where(valid[None, None, :, None, :, None], s, NEG_INF)
    pr = jax.nn.softmax(s.reshape(B, H, NC, CHUNK, BAND * CHUNK), axis=-1)
    pr = pr.reshape(B, H, NC, CHUNK, BAND, CHUNK).astype(v.dtype)
    o = functools.reduce(
        lambda acc, t: acc + t,
        [jnp.einsum('bhcqk,bckhd->bcqhd', pr[:, :, :, :, j], vc[:, j:j + NC]) for j in range(BAND)])
    return o.reshape(B, T, H * Dh)


def conv_module(a, g, w_dw, b_dw, ln_g, ln_b, w_pw, b_pw):
    u = a * jax.nn.sigmoid(g)
    u = jnp.pad(u, ((0, 0), (CONV_K - 1, 0), (0, 0)))
    u = lax.conv_general_dilated(
        u, w_dw[:, None, :], window_strides=(1,), padding='VALID',
        dimension_numbers=('NWC', 'WIO', 'NWC'),
        feature_group_count=CONV_WIDTH) + b_dw
    u = jax.nn.silu(layer_norm(u, ln_g, ln_b))
    return u @ w_pw + b_pw


def _fwd_setup_inputs(seed: int = 0) -> dict:
    key = jax.random.key(seed)
    ks = jax.random.split(key, 20)
    n = jax.random.normal
    f = jnp.float32
    return {
        "x": n(ks[0], (BATCH, SEQ, D_MODEL), f),
        "p": n(ks[1], (DEPTH, BATCH, SEQ, PLE_DIM), f),
        "norm_in_g": 1.0 + 0.05 * n(ks[2], (DEPTH, D_MODEL), f),
        "w_in": n(ks[3], (DEPTH, D_MODEL, IN_COLS), f) * D_MODEL ** -0.5,
        "rel_table": 0.1 * n(ks[4], (DEPTH, N_HEADS, 2 * MAX_REL + 1), f),
        "w_dw": n(ks[5], (DEPTH, CONV_K, CONV_WIDTH), f) * CONV_K ** -0.5,
        "b_dw": 0.02 * n(ks[6], (DEPTH, CONV_WIDTH), f),
        "conv_ln_g": 1.0 + 0.05 * n(ks[7], (DEPTH, CONV_WIDTH), f),
        "conv_ln_b": 0.02 * n(ks[8], (DEPTH, CONV_WIDTH), f),
        "w_pw": n(ks[9], (DEPTH, CONV_WIDTH, CONV_WIDTH), f) * CONV_WIDTH ** -0.5,
        "b_pw": 0.02 * n(ks[10], (DEPTH, CONV_WIDTH), f),
        "attn_out_g": 1.0 + 0.05 * n(ks[11], (DEPTH, ATT_WIDTH), f),
        "conv_out_g": 1.0 + 0.05 * n(ks[12], (DEPTH, CONV_WIDTH), f),
        "w_out": n(ks[13], (DEPTH, D_MODEL, D_MODEL), f) * D_MODEL ** -0.5,
        "ple_norm_g": 1.0 + 0.05 * n(ks[14], (DEPTH, D_MODEL), f),
        "w_ple_gate": n(ks[15], (DEPTH, D_MODEL, D_MODEL), f) * D_MODEL ** -0.5,
        "b_ple_gate": 0.02 * n(ks[16], (DEPTH, D_MODEL), f),
        "w_ple": n(ks[17], (DEPTH, PLE_DIM, D_MODEL), f) * PLE_DIM ** -0.5,
        "final_g": 1.0 + 0.05 * n(ks[18], (D_MODEL,), f),
    }


def _fwd_reference(x, p, norm_in_g, w_in, rel_table, w_dw, b_dw, conv_ln_g, conv_ln_b,
              w_pw, b_pw, attn_out_g, conv_out_g, w_out, ple_norm_g, w_ple_gate,
              b_ple_gate, w_ple, final_g):
    B, T, _ = x.shape
    h = x
    for i in range(DEPTH):
        xn = rms_norm(h, norm_in_g[i])
        proj = xn @ w_in[i]
        q, k, v, z_a, a_c, g_c, z_c = jnp.split(
            proj, np.cumsum([ATT_WIDTH] * 4 + [CONV_WIDTH] * 2)[:6].tolist(), axis=-1)
        shp = (B, T, N_HEADS, HEAD_DIM)
        y_a = chunked_attention(q.reshape(shp), k.reshape(shp), v.reshape(shp), rel_table[i])
        y_a = rms_norm(y_a, attn_out_g[i]) * jax.nn.silu(z_a)
        y_c = conv_module(a_c, g_c, w_dw[i], b_dw[i], conv_ln_g[i], conv_ln_b[i], w_pw[i], b_pw[i])
        y_c = rms_norm(y_c, conv_out_g[i]) * jax.nn.silu(z_c)
        h = h + jnp.concatenate([y_a, y_c], axis=-1) @ w_out[i]
        gate = jax.nn.sigmoid(rms_norm(h, ple_norm_g[i]) @ w_ple_gate[i] + b_ple_gate[i])
        h = h + gate * (p[i] @ w_ple[i])
    return rms_norm(h, final_g)


import jax as _jax
import jax.numpy as _jnp

TWIN_FORMAT = 'train_step'
FWD_PARAMS = ['x', 'p', 'norm_in_g', 'w_in', 'rel_table', 'w_dw', 'b_dw', 'conv_ln_g', 'conv_ln_b', 'w_pw', 'b_pw', 'attn_out_g', 'conv_out_g', 'w_out', 'ple_norm_g', 'w_ple_gate', 'b_ple_gate', 'w_ple', 'final_g']
TWIN_WEIGHTS = ['norm_in_g', 'w_in', 'rel_table', 'w_dw', 'b_dw', 'conv_ln_g', 'conv_ln_b', 'w_pw', 'b_pw', 'attn_out_g', 'conv_out_g', 'w_out', 'ple_norm_g', 'w_ple_gate', 'b_ple_gate', 'w_ple', 'final_g']
TWIN_DIFF_INPUT = 'x'
TWIN_INPUTS = ['x', 'p', 'norm_in_g', 'w_in', 'rel_table', 'w_dw', 'b_dw', 'conv_ln_g', 'conv_ln_b', 'w_pw', 'b_pw', 'attn_out_g', 'conv_out_g', 'w_out', 'ple_norm_g', 'w_ple_gate', 'b_ple_gate', 'w_ple', 'final_g', 'loss_target', 'm_norm_in_g', 'm_w_in', 'm_rel_table', 'm_w_dw', 'm_b_dw', 'm_conv_ln_g', 'm_conv_ln_b', 'm_w_pw', 'm_b_pw', 'm_attn_out_g', 'm_conv_out_g', 'm_w_out', 'm_ple_norm_g', 'm_w_ple_gate', 'm_b_ple_gate', 'm_w_ple', 'm_final_g', 'v_norm_in_g', 'v_w_in', 'v_rel_table', 'v_w_dw', 'v_b_dw', 'v_conv_ln_g', 'v_conv_ln_b', 'v_w_pw', 'v_b_pw', 'v_attn_out_g', 'v_conv_out_g', 'v_w_out', 'v_ple_norm_g', 'v_w_ple_gate', 'v_b_ple_gate', 'v_w_ple', 'v_final_g']
TWIN_OUTPUTS = ['loss', 'grad_x', 'grad_norm_in_g', 'grad_w_in', 'grad_rel_table', 'grad_w_dw', 'grad_b_dw', 'grad_conv_ln_g', 'grad_conv_ln_b', 'grad_w_pw', 'grad_b_pw', 'grad_attn_out_g', 'grad_conv_out_g', 'grad_w_out', 'grad_ple_norm_g', 'grad_w_ple_gate', 'grad_b_ple_gate', 'grad_w_ple', 'grad_final_g', 'delta_norm_in_g', 'delta_w_in', 'delta_rel_table', 'delta_w_dw', 'delta_b_dw', 'delta_conv_ln_g', 'delta_conv_ln_b', 'delta_w_pw', 'delta_b_pw', 'delta_attn_out_g', 'delta_conv_out_g', 'delta_w_out', 'delta_ple_norm_g', 'delta_w_ple_gate', 'delta_b_ple_gate', 'delta_w_ple', 'delta_final_g', 'new_m_norm_in_g', 'new_m_w_in', 'new_m_rel_table', 'new_m_w_dw', 'new_m_b_dw', 'new_m_conv_ln_g', 'new_m_conv_ln_b', 'new_m_w_pw', 'new_m_b_pw', 'new_m_attn_out_g', 'new_m_conv_out_g', 'new_m_w_out', 'new_m_ple_norm_g', 'new_m_w_ple_gate', 'new_m_b_ple_gate', 'new_m_w_ple', 'new_m_final_g', 'new_v_norm_in_g', 'new_v_w_in', 'new_v_rel_table', 'new_v_w_dw', 'new_v_b_dw', 'new_v_conv_ln_g', 'new_v_conv_ln_b', 'new_v_w_pw', 'new_v_b_pw', 'new_v_attn_out_g', 'new_v_conv_out_g', 'new_v_w_out', 'new_v_ple_norm_g', 'new_v_w_ple_gate', 'new_v_b_ple_gate', 'new_v_w_ple', 'new_v_final_g']
TWIN_LEAF_KINDS = {'loss': 'loss', 'grad_x': 'grad_x', 'grad_norm_in_g': 'grad_w', 'grad_w_in': 'grad_w', 'grad_rel_table': 'grad_w', 'grad_w_dw': 'grad_w', 'grad_b_dw': 'grad_w', 'grad_conv_ln_g': 'grad_w', 'grad_conv_ln_b': 'grad_w', 'grad_w_pw': 'grad_w', 'grad_b_pw': 'grad_w', 'grad_attn_out_g': 'grad_w', 'grad_conv_out_g': 'grad_w', 'grad_w_out': 'grad_w', 'grad_ple_norm_g': 'grad_w', 'grad_w_ple_gate': 'grad_w', 'grad_b_ple_gate': 'grad_w', 'grad_w_ple': 'grad_w', 'grad_final_g': 'grad_w', 'delta_norm_in_g': 'delta_w', 'delta_w_in': 'delta_w', 'delta_rel_table': 'delta_w', 'delta_w_dw': 'delta_w', 'delta_b_dw': 'delta_w', 'delta_conv_ln_g': 'delta_w', 'delta_conv_ln_b': 'delta_w', 'delta_w_pw': 'delta_w', 'delta_b_pw': 'delta_w', 'delta_attn_out_g': 'delta_w', 'delta_conv_out_g': 'delta_w', 'delta_w_out': 'delta_w', 'delta_ple_norm_g': 'delta_w', 'delta_w_ple_gate': 'delta_w', 'delta_b_ple_gate': 'delta_w', 'delta_w_ple': 'delta_w', 'delta_final_g': 'delta_w', 'new_m_norm_in_g': 'new_m', 'new_m_w_in': 'new_m', 'new_m_rel_table': 'new_m', 'new_m_w_dw': 'new_m', 'new_m_b_dw': 'new_m', 'new_m_conv_ln_g': 'new_m', 'new_m_conv_ln_b': 'new_m', 'new_m_w_pw': 'new_m', 'new_m_b_pw': 'new_m', 'new_m_attn_out_g': 'new_m', 'new_m_conv_out_g': 'new_m', 'new_m_w_out': 'new_m', 'new_m_ple_norm_g': 'new_m', 'new_m_w_ple_gate': 'new_m', 'new_m_b_ple_gate': 'new_m', 'new_m_w_ple': 'new_m', 'new_m_final_g': 'new_m', 'new_v_norm_in_g': 'new_v', 'new_v_w_in': 'new_v', 'new_v_rel_table': 'new_v', 'new_v_w_dw': 'new_v', 'new_v_b_dw': 'new_v', 'new_v_conv_ln_g': 'new_v', 'new_v_conv_ln_b': 'new_v', 'new_v_w_pw': 'new_v', 'new_v_b_pw': 'new_v', 'new_v_attn_out_g': 'new_v', 'new_v_conv_out_g': 'new_v', 'new_v_w_out': 'new_v', 'new_v_ple_norm_g': 'new_v', 'new_v_w_ple_gate': 'new_v', 'new_v_b_ple_gate': 'new_v', 'new_v_w_ple': 'new_v', 'new_v_final_g': 'new_v'}


def _forward(args):
    return _fwd_reference(*[args[k] for k in FWD_PARAMS])


def _output_shape():
    out = _jax.eval_shape(lambda: _forward(_fwd_setup_inputs(0)))
    return out.shape, out.dtype

N_MICROBATCH = 1
ADAM_LR = 0.001
ADAM_B1 = 0.9
ADAM_B2 = 0.999
ADAM_EPS = 1e-08
ADAM_WD = 0.01
ADAM_STEP = 10
PER_EXAMPLE_BATCH_AXIS = {'x': 0, 'p': 1, 'loss_target': 0}
SHARED_INPUTS = []
_WEIGHT_DTYPES = {'norm_in_g': _jnp.float32, 'w_in': _jnp.float32, 'rel_table': _jnp.float32, 'w_dw': _jnp.float32, 'b_dw': _jnp.float32, 'conv_ln_g': _jnp.float32, 'conv_ln_b': _jnp.float32, 'w_pw': _jnp.float32, 'b_pw': _jnp.float32, 'attn_out_g': _jnp.float32, 'conv_out_g': _jnp.float32, 'w_out': _jnp.float32, 'ple_norm_g': _jnp.float32, 'w_ple_gate': _jnp.float32, 'b_ple_gate': _jnp.float32, 'w_ple': _jnp.float32, 'final_g': _jnp.float32}
MOMENT_SCALE = {'norm_in_g': 4.014610e-02, 'w_in': 2.032029e-02, 'rel_table': 9.950681e-03, 'w_dw': 2.177834e-02, 'b_dw': 4.606711e-02, 'conv_ln_g': 2.549316e-02, 'conv_ln_b': 2.217840e-02, 'w_pw': 2.114387e-02, 'b_pw': 3.572492e-02, 'attn_out_g': 2.140629e-02, 'conv_out_g': 2.167004e-02, 'w_out': 2.120295e-02, 'ple_norm_g': 7.347346e-03, 'w_ple_gate': 7.484982e-03, 'b_ple_gate': 4.188192e-02, 'w_ple': 1.941442e-02, 'final_g': 8.003691e+00}


def _to_microbatches(a, axis):
    t = _jnp.moveaxis(a, axis, 0)
    t = t.reshape((N_MICROBATCH, t.shape[0] // N_MICROBATCH) + t.shape[1:])
    return _jnp.moveaxis(t, 1, axis + 1)


def setup_inputs(seed: int = 0) -> dict:
    inp = _fwd_setup_inputs(seed)
    key = _jax.random.fold_in(_jax.random.key(seed), 7919)
    shape, _ = _output_shape()
    out = dict(inp)
    out["loss_target"] = _jax.random.normal(_jax.random.fold_in(key, 0), shape, _jnp.float32)
    for i, name in enumerate(TWIN_WEIGHTS):
        w = inp[name].astype(_jnp.float32)
        if MOMENT_SCALE is None:
            s = _jnp.sqrt(_jnp.mean(_jnp.square(w)) + 1e-30)
        else:
            s = MOMENT_SCALE[name]
        km, kv = _jax.random.split(_jax.random.fold_in(key, i + 1))
        out[name] = w
        out["m_" + name] = s * _jax.random.normal(km, w.shape, _jnp.float32)
        out["v_" + name] = (s * s) * _jax.random.uniform(kv, w.shape, _jnp.float32, 0.5, 1.5)
    if N_MICROBATCH > 1:
        for name, axis in PER_EXAMPLE_BATCH_AXIS.items():
            out[name] = _to_microbatches(out[name], axis)
    return {'x': out['x'], 'p': out['p'], 'norm_in_g': out['norm_in_g'], 'w_in': out['w_in'], 'rel_table': out['rel_table'], 'w_dw': out['w_dw'], 'b_dw': out['b_dw'], 'conv_ln_g': out['conv_ln_g'], 'conv_ln_b': out['conv_ln_b'], 'w_pw': out['w_pw'], 'b_pw': out['b_pw'], 'attn_out_g': out['attn_out_g'], 'conv_out_g': out['conv_out_g'], 'w_out': out['w_out'], 'ple_norm_g': out['ple_norm_g'], 'w_ple_gate': out['w_ple_gate'], 'b_ple_gate': out['b_ple_gate'], 'w_ple': out['w_ple'], 'final_g': out['final_g'], 'loss_target': out['loss_target'], 'm_norm_in_g': out['m_norm_in_g'], 'm_w_in': out['m_w_in'], 'm_rel_table': out['m_rel_table'], 'm_w_dw': out['m_w_dw'], 'm_b_dw': out['m_b_dw'], 'm_conv_ln_g': out['m_conv_ln_g'], 'm_conv_ln_b': out['m_conv_ln_b'], 'm_w_pw': out['m_w_pw'], 'm_b_pw': out['m_b_pw'], 'm_attn_out_g': out['m_attn_out_g'], 'm_conv_out_g': out['m_conv_out_g'], 'm_w_out': out['m_w_out'], 'm_ple_norm_g': out['m_ple_norm_g'], 'm_w_ple_gate': out['m_w_ple_gate'], 'm_b_ple_gate': out['m_b_ple_gate'], 'm_w_ple': out['m_w_ple'], 'm_final_g': out['m_final_g'], 'v_norm_in_g': out['v_norm_in_g'], 'v_w_in': out['v_w_in'], 'v_rel_table': out['v_rel_table'], 'v_w_dw': out['v_w_dw'], 'v_b_dw': out['v_b_dw'], 'v_conv_ln_g': out['v_conv_ln_g'], 'v_conv_ln_b': out['v_conv_ln_b'], 'v_w_pw': out['v_w_pw'], 'v_b_pw': out['v_b_pw'], 'v_attn_out_g': out['v_attn_out_g'], 'v_conv_out_g': out['v_conv_out_g'], 'v_w_out': out['v_w_out'], 'v_ple_norm_g': out['v_ple_norm_g'], 'v_w_ple_gate': out['v_w_ple_gate'], 'v_b_ple_gate': out['v_b_ple_gate'], 'v_w_ple': out['v_w_ple'], 'v_final_g': out['v_final_g']}


def _loss(weights, diff, rest, loss_target):
    with _jax.named_scope("forward"):
        args = {**rest, TWIN_DIFF_INPUT: diff, **{k: w.astype(_WEIGHT_DTYPES[k]) for k, w in weights.items()}}
        y = _forward(args)
    with _jax.named_scope("loss_head"):
        err = _jnp.square(y.astype(_jnp.float32) - loss_target)
        return 0.5 * _jnp.sum(_jnp.mean(err, axis=-1)) if err.ndim else 0.5 * err


def _adamw(w, g, m, v):
    m = ADAM_B1 * m + (1.0 - ADAM_B1) * g
    v = ADAM_B2 * v + (1.0 - ADAM_B2) * _jnp.square(g)
    m_hat = m / (1.0 - ADAM_B1 ** ADAM_STEP)
    v_hat = v / (1.0 - ADAM_B2 ** ADAM_STEP)
    delta = -ADAM_LR * (m_hat / (_jnp.sqrt(v_hat) + ADAM_EPS) + ADAM_WD * w)
    return delta, m, v


def reference(x, p, norm_in_g, w_in, rel_table, w_dw, b_dw, conv_ln_g, conv_ln_b, w_pw, b_pw, attn_out_g, conv_out_g, w_out, ple_norm_g, w_ple_gate, b_ple_gate, w_ple, final_g, loss_target, m_norm_in_g, m_w_in, m_rel_table, m_w_dw, m_b_dw, m_conv_ln_g, m_conv_ln_b, m_w_pw, m_b_pw, m_attn_out_g, m_conv_out_g, m_w_out, m_ple_norm_g, m_w_ple_gate, m_b_ple_gate, m_w_ple, m_final_g, v_norm_in_g, v_w_in, v_rel_table, v_w_dw, v_b_dw, v_conv_ln_g, v_conv_ln_b, v_w_pw, v_b_pw, v_attn_out_g, v_conv_out_g, v_w_out, v_ple_norm_g, v_w_ple_gate, v_b_ple_gate, v_w_ple, v_final_g):
    given = dict(x=x, p=p, norm_in_g=norm_in_g, w_in=w_in, rel_table=rel_table, w_dw=w_dw, b_dw=b_dw, conv_ln_g=conv_ln_g, conv_ln_b=conv_ln_b, w_pw=w_pw, b_pw=b_pw, attn_out_g=attn_out_g, conv_out_g=conv_out_g, w_out=w_out, ple_norm_g=ple_norm_g, w_ple_gate=w_ple_gate, b_ple_gate=b_ple_gate, w_ple=w_ple, final_g=final_g, loss_target=loss_target, m_norm_in_g=m_norm_in_g, m_w_in=m_w_in, m_rel_table=m_rel_table, m_w_dw=m_w_dw, m_b_dw=m_b_dw, m_conv_ln_g=m_conv_ln_g, m_conv_ln_b=m_conv_ln_b, m_w_pw=m_w_pw, m_b_pw=m_b_pw, m_attn_out_g=m_attn_out_g, m_conv_out_g=m_conv_out_g, m_w_out=m_w_out, m_ple_norm_g=m_ple_norm_g, m_w_ple_gate=m_w_ple_gate, m_b_ple_gate=m_b_ple_gate, m_w_ple=m_w_ple, m_final_g=m_final_g, v_norm_in_g=v_norm_in_g, v_w_in=v_w_in, v_rel_table=v_rel_table, v_w_dw=v_w_dw, v_b_dw=v_b_dw, v_conv_ln_g=v_conv_ln_g, v_conv_ln_b=v_conv_ln_b, v_w_pw=v_w_pw, v_b_pw=v_b_pw, v_attn_out_g=v_attn_out_g, v_conv_out_g=v_conv_out_g, v_w_out=v_w_out, v_ple_norm_g=v_ple_norm_g, v_w_ple_gate=v_w_ple_gate, v_b_ple_gate=v_b_ple_gate, v_w_ple=v_w_ple, v_final_g=v_final_g)
    weights = {n: given[n] for n in TWIN_WEIGHTS}
    shared = {n: given[n] for n in SHARED_INPUTS}
    per_example = {n: given[n] for n in ['x', 'p']}
    grad_fn = _jax.value_and_grad(_loss, argnums=(0, 1))

    def one_microbatch(ex, loss_target):
        ex = dict(ex)
        diff = ex.pop(TWIN_DIFF_INPUT)
        return grad_fn(weights, diff, {**shared, **ex}, loss_target)

    if N_MICROBATCH == 1:
        loss, (grad_w, grad_x) = one_microbatch(per_example, given["loss_target"])
    else:
        def body(carry, xs):
            loss_sum, grad_sum = carry
            l_k, (gw_k, gx_k) = one_microbatch(xs[0], xs[1])
            with _jax.named_scope("update"):
                return (loss_sum + l_k, _jax.tree.map(_jnp.add, grad_sum, gw_k)), gx_k

        init = (_jnp.zeros((), _jnp.float32), _jax.tree.map(_jnp.zeros_like, weights))
        (loss, grad_w), grad_x = _jax.lax.scan(body, init, (per_example, given["loss_target"]))
    with _jax.named_scope("update"):
        delta_w, new_m, new_v = {}, {}, {}
        for n in TWIN_WEIGHTS:
            delta_w[n], new_m[n], new_v[n] = _adamw(weights[n], grad_w[n], given["m_" + n], given["v_" + n])
    return (loss, grad_x, *[grad_w[n] for n in TWIN_WEIGHTS], *[delta_w[n] for n in TWIN_WEIGHTS],
            *[new_m[n] for n in TWIN_WEIGHTS], *[new_v[n] for n in TWIN_WEIGHTS])
```

```python
import functools

import numpy as np
import jax
import jax.numpy as jnp
from jax import lax
from jax.experimental import pallas as pl
from jax.experimental.pallas import tpu as pltpu

F32 = jnp.float32
BF16 = jnp.bfloat16
MESH = pl.DeviceIdType.MESH

CHUNK = 64
LEFT_CHUNKS = 8
HEAD_DIM = 128
MAX_REL = 256
CONV_K = 31
EPS = 1e-6
NEG_INF = -1e30
ADAM_LR = 0.001
ADAM_B1 = 0.9
ADAM_B2 = 0.999
ADAM_EPS = 1e-08
ADAM_WD = 0.01
ADAM_STEP = 10

N_DEV = 8
QB = CHUNK * LEFT_CHUNKS
KB = 2 * QB
HALO = 32
CONV_COLS = 256
CONV_ROWS = 64
LANES = 128
VMEM_CAP = 60 * 1024 * 1024
NT = (((1,), (1,)), ((), ()))
TN = (((0,), (0,)), ((), ()))
NN = (((1,), (0,)), ((), ()))


def _cparams(sem, est_bytes):
    return pltpu.CompilerParams(dimension_semantics=sem, vmem_limit_bytes=int(min(VMEM_CAP, max(32 << 20, 2 * est_bytes))))


def _nbytes(shape, dtype):
    return int(np.prod(shape)) * jnp.dtype(dtype).itemsize


def _row_block(rows, bytes_per_row, budget=12 << 20):
    tb = rows
    while tb > 16 and (tb * bytes_per_row * 2 > budget or rows % tb):
        tb //= 2
    assert rows % tb == 0
    return tb


def _sigmoid(x):
    return 1.0 / (1.0 + jnp.exp(-x))


def _cast_bf16(w, name):
    r, c = w.shape
    tb = _row_block(r, c * 6)

    def body(w_ref, o_ref):
        o_ref[...] = w_ref[...].astype(BF16)

    return pl.pallas_call(
        body, name=name, grid=(r // tb,),
        in_specs=[pl.BlockSpec((tb, c), lambda i: (i, 0))], out_specs=pl.BlockSpec((tb, c), lambda i: (i, 0)),
        out_shape=jax.ShapeDtypeStruct((r, c), BF16), compiler_params=_cparams(("parallel",), tb * c * 12))(w)


def _rms_fwd(x, g, name):
    t, d = x.shape
    tb = _row_block(t, d * 6)

    def body(x_ref, g_ref, o_ref):
        xv = x_ref[...]
        r = lax.rsqrt(jnp.mean(xv * xv, axis=-1, keepdims=True) + EPS)
        o_ref[...] = (xv * r * g_ref[...]).astype(BF16)

    return pl.pallas_call(
        body, name=name, grid=(t // tb,),
        in_specs=[pl.BlockSpec((tb, d), lambda i: (i, 0)), pl.BlockSpec((1, d), lambda i: (0, 0))],
        out_specs=pl.BlockSpec((tb, d), lambda i: (i, 0)),
        out_shape=jax.ShapeDtypeStruct((t, d), BF16), compiler_params=_cparams(("parallel",), tb * d * 12))(x, g)


def _rms_bwd(dy, x, g, resid, name, with_bf16):
    t, d = x.shape
    tb = _row_block(t, d * 18)

    def body(dy_ref, x_ref, g_ref, res_ref, *outs):
        i = pl.program_id(0)
        xv = x_ref[...]
        r = lax.rsqrt(jnp.mean(xv * xv, axis=-1, keepdims=True) + EPS)
        n = xv * r
        dyv = dy_ref[...].astype(F32)
        dn = dyv * g_ref[...]
        dx = res_ref[...] + r * (dn - n * jnp.mean(dn * n, axis=-1, keepdims=True))
        outs[0][...] = dx
        if with_bf16:
            outs[1][...] = dx.astype(BF16)
        dg_ref = outs[-1]
        part = jnp.sum(dyv * n, axis=0, keepdims=True)

        @pl.when(i == 0)
        def _():
            dg_ref[...] = part

        @pl.when(i > 0)
        def _():
            dg_ref[...] += part

    row = pl.BlockSpec((tb, d), lambda i: (i, 0))
    vec = pl.BlockSpec((1, d), lambda i: (0, 0))
    out_shape = [jax.ShapeDtypeStruct((t, d), F32)] + ([jax.ShapeDtypeStruct((t, d), BF16)] if with_bf16 else []) + [
        jax.ShapeDtypeStruct((1, d), F32)]
    out_specs = [row] + ([row] if with_bf16 else []) + [vec]
    return pl.pallas_call(
        body, name=name, grid=(t // tb,), in_specs=[row, row, vec, row], out_specs=out_specs, out_shape=out_shape,
        compiler_params=_cparams(("arbitrary",), tb * d * 36))(dy, x, g, resid)


def _final(h2, tgt, gate, pe, g):
    t, d = h2.shape
    tb = _row_block(t, d * 20)

    def body(h_ref, t_ref, gate_ref, pe_ref, g_ref, dh_ref, dpe_ref, dgl_ref, loss_ref, dfg_ref, dbg_ref):
        i = pl.program_id(0)
        hv = h_ref[...]
        r = lax.rsqrt(jnp.mean(hv * hv, axis=-1, keepdims=True) + EPS)
        n = hv * r
        gv = g_ref[...]
        err = n * gv - t_ref[...]
        loss_part = 0.5 * jnp.sum(jnp.mean(err * err, axis=-1, keepdims=True), axis=0, keepdims=True)
        dy = err * (1.0 / d)
        dn = dy * gv
        dh = r * (dn - n * jnp.mean(dn * n, axis=-1, keepdims=True))
        dh_ref[...] = dh
        gt = gate_ref[...].astype(F32)
        dpe_ref[...] = (dh * gt).astype(BF16)
        dgl = dh * pe_ref[...].astype(F32) * gt * (1.0 - gt)
        dgl_ref[...] = dgl.astype(BF16)
        dfg = jnp.sum(dy * n, axis=0, keepdims=True)
        dbg = jnp.sum(dgl, axis=0, keepdims=True)

        @pl.when(i == 0)
        def _():
            loss_ref[...] = jnp.broadcast_to(loss_part, loss_ref.shape)
            dfg_ref[...] = dfg
            dbg_ref[...] = dbg

        @pl.when(i > 0)
        def _():
            loss_ref[...] += jnp.broadcast_to(loss_part, loss_ref.shape)
            dfg_ref[...] += dfg
            dbg_ref[...] += dbg

    row = pl.BlockSpec((tb, d), lambda i: (i, 0))
    vec = pl.BlockSpec((1, d), lambda i: (0, 0))
    return pl.pallas_call(
        body, name="final_loss_bwd", grid=(t // tb,), in_specs=[row, row, row, row, vec],
        out_specs=[row, row, row, pl.BlockSpec((1, LANES), lambda i: (0, 0)), vec, vec],
        out_shape=[jax.ShapeDtypeStruct((t, d), F32), jax.ShapeDtypeStruct((t, d), BF16), jax.ShapeDtypeStruct((t, d), BF16),
                   jax.ShapeDtypeStruct((1, LANES), F32), jax.ShapeDtypeStruct((1, d), F32), jax.ShapeDtypeStruct((1, d), F32)],
        compiler_params=_cparams(("arbitrary",), tb * d * 40))(h2, tgt, gate, pe, g)


def _post_fwd(o, yc, proj7, gains):
    t, a = o.shape
    tb = _row_block(t, a * 14)
    nb = t // tb

    def body(o_ref, yc_ref, z_ref, g_ref, out_ref):
        s = pl.program_id(0)
        xin = jnp.where(s == 0, o_ref[...], yc_ref[...])
        r = lax.rsqrt(jnp.mean(xin * xin, axis=-1, keepdims=True) + EPS)
        z = z_ref[...].astype(F32)
        out_ref[...] = (xin * r * g_ref[...] * (z * _sigmoid(z))).astype(BF16)

    return pl.pallas_call(
        body, name="post_fwd", grid=(2, nb),
        in_specs=[pl.BlockSpec((tb, a), lambda s, i: (i * (1 - s) + (nb - 1) * s, 0)),
                  pl.BlockSpec((tb, a), lambda s, i: (i * s, 0)),
                  pl.BlockSpec((None, tb, a), lambda s, i: (3 + 3 * s, i, 0)),
                  pl.BlockSpec((None, 1, a), lambda s, i: (s, 0, 0))],
        out_specs=pl.BlockSpec((tb, a), lambda s, i: (i, s)),
        out_shape=jax.ShapeDtypeStruct((t, 2 * a), BF16),
        compiler_params=_cparams(("arbitrary", "arbitrary"), tb * a * 28))(o, yc, proj7, gains)


def _post_bwd(dycat, o, yc, proj7, gains):
    t, a = o.shape
    tb = _row_block(t, a * 24)
    nb = t // tb

    def body(dy_ref, o_ref, yc_ref, z_ref, g_ref, dz_ref, dx_ref, dg_ref, cs_ref):
        s = pl.program_id(0)
        i = pl.program_id(1)
        xin = jnp.where(s == 0, o_ref[...], yc_ref[...])
        r = lax.rsqrt(jnp.mean(xin * xin, axis=-1, keepdims=True) + EPS)
        n = xin * r
        z = z_ref[...].astype(F32)
        sg = _sigmoid(z)
        gv = g_ref[...]
        dy = dy_ref[...]
        dz_ref[...] = (dy * (n * gv) * (sg * (1.0 + z * (1.0 - sg)))).astype(BF16)
        dyn = dy * (z * sg)
        dn = dyn * gv
        dx = r * (dn - n * jnp.mean(dn * n, axis=-1, keepdims=True))
        dxb = dx.astype(BF16)
        dx_ref[...] = dxb
        dg = jnp.sum(dyn * n, axis=0, keepdims=True)
        cs = jnp.sum(dxb.astype(F32), axis=0, keepdims=True)

        @pl.when(i == 0)
        def _():
            dg_ref[...] = dg
            cs_ref[...] = cs

        @pl.when(i > 0)
        def _():
            dg_ref[...] += dg
            cs_ref[...] += cs

    vec = pl.BlockSpec((None, 1, a), lambda s, i: (s, 0, 0))
    return pl.pallas_call(
        body, name="post_bwd", grid=(2, nb),
        in_specs=[pl.BlockSpec((tb, a), lambda s, i: (i, s)),
                  pl.BlockSpec((tb, a), lambda s, i: (i * (1 - s) + (nb - 1) * s, 0)),
                  pl.BlockSpec((tb, a), lambda s, i: (i * s, 0)),
                  pl.BlockSpec((None, tb, a), lambda s, i: (3 + 3 * s, i, 0)),
                  vec],
        out_specs=[pl.BlockSpec((None, tb, a), lambda s, i: (3 + 3 * s, i, 0)),
                   pl.BlockSpec((None, tb, a), lambda s, i: (s, i, 0)), vec, vec],
        out_shape=[jax.ShapeDtypeStruct((7, t, a), BF16), jax.ShapeDtypeStruct((2, t, a), BF16),
                   jax.ShapeDtypeStruct((2, 1, a), F32), jax.ShapeDtypeStruct((2, 1, a), F32)],
        compiler_params=_cparams(("arbitrary", "arbitrary"), tb * a * 48))(dycat, o, yc, proj7, gains)


def _ln_silu_fwd(u1, g, b):
    t, c = u1.shape
    tb = _row_block(t, c * 6)

    def body(u_ref, g_ref, b_ref, o_ref):
        u = u_ref[...]
        mu = jnp.mean(u, axis=-1, keepdims=True)
        xc = u - mu
        rstd = lax.rsqrt(jnp.mean(xc * xc, axis=-1, keepdims=True) + EPS)
        u2 = xc * rstd * g_ref[...] + b_ref[...]
        o_ref[...] = (u2 * _sigmoid(u2)).astype(BF16)

    row = pl.BlockSpec((tb, c), lambda i: (i, 0))
    vec = pl.BlockSpec((1, c), lambda i: (0, 0))
    return pl.pallas_call(
        body, name="ln_silu_fwd", grid=(t // tb,), in_specs=[row, vec, vec], out_specs=row,
        out_shape=jax.ShapeDtypeStruct((t, c), BF16), compiler_params=_cparams(("parallel",), tb * c * 12))(u1, g, b)


def _ln_silu_bwd(du3, u1, g, b):
    t, c = u1.shape
    tb = _row_block(t, c * 12)

    def body(d_ref, u_ref, g_ref, b_ref, du_ref, dg_ref, db_ref, cs_ref):
        i = pl.program_id(0)
        u = u_ref[...]
        mu = jnp.mean(u, axis=-1, keepdims=True)
        xc = u - mu
        rstd = lax.rsqrt(jnp.mean(xc * xc, axis=-1, keepdims=True) + EPS)
        xh = xc * rstd
        gv = g_ref[...]
        u2 = xh * gv + b_ref[...]
        sg = _sigmoid(u2)
        du2 = d_ref[...] * (sg * (1.0 + u2 * (1.0 - sg)))
        dxh = du2 * gv
        du1 = rstd * (dxh - jnp.mean(dxh, axis=-1, keepdims=True) - xh * jnp.mean(dxh * xh, axis=-1, keepdims=True))
        du_ref[...] = du1
        dg = jnp.sum(du2 * xh, axis=0, keepdims=True)
        db = jnp.sum(du2, axis=0, keepdims=True)
        cs = jnp.sum(du1, axis=0, keepdims=True)

        @pl.when(i == 0)
        def _():
            dg_ref[...] = dg
            db_ref[...] = db
            cs_ref[...] = cs

        @pl.when(i > 0)
        def _():
            dg_ref[...] += dg
            db_ref[...] += db
            cs_ref[...] += cs

    row = pl.BlockSpec((tb, c), lambda i: (i, 0))
    vec = pl.BlockSpec((1, c), lambda i: (0, 0))
    vs = jax.ShapeDtypeStruct((1, c), F32)
    return pl.pallas_call(
        body, name="ln_silu_bwd", grid=(t // tb,), in_specs=[row, row, vec, vec], out_specs=[row, vec, vec, vec],
        out_shape=[jax.ShapeDtypeStruct((t, c), F32), vs, vs, vs],
        compiler_params=_cparams(("arbitrary",), tb * c * 24))(du3, u1, g, b)


def _conv_block(t):
    tb = 512
    while t % tb:
        tb //= 2
    assert tb >= HALO
    return tb


def _conv_fwd(proj7, w_dw, b_dw):
    _, t, c = proj7.shape
    tb = _conv_block(t)
    hb = tb // HALO
    cw = CONV_COLS

    def body(a_ref, g_ref, ah_ref, gh_ref, w_ref, b_ref, u1_ref, ext):
        i = pl.program_id(1)
        halo = ah_ref[...].astype(F32) * _sigmoid(gh_ref[...].astype(F32))
        ext[0:HALO, :] = jnp.where(i > 0, halo, 0.0)
        ext[HALO:HALO + tb, :] = a_ref[...].astype(F32) * _sigmoid(g_ref[...].astype(F32))
        bias = b_ref[...]
        for r0 in range(0, tb, CONV_ROWS):
            acc = jnp.broadcast_to(bias, (CONV_ROWS, cw))
            for k in range(CONV_K):
                off = r0 + HALO - (CONV_K - 1) + k
                acc = acc + ext[off:off + CONV_ROWS, :] * w_ref[k:k + 1, :]
            u1_ref[r0:r0 + CONV_ROWS, :] = acc

    blk = lambda seg: pl.BlockSpec((None, tb, cw), lambda j, i: (seg, i, j))
    halo = lambda seg: pl.BlockSpec((None, HALO, cw), lambda j, i: (seg, jnp.maximum(i * hb - 1, 0), j))
    return pl.pallas_call(
        body, name="conv_fwd", grid=(c // cw, t // tb),
        in_specs=[blk(4), blk(5), halo(4), halo(5), pl.BlockSpec((32, cw), lambda j, i: (0, j)),
                  pl.BlockSpec((1, cw), lambda j, i: (0, j))],
        out_specs=pl.BlockSpec((tb, cw), lambda j, i: (i, j)),
        out_shape=jax.ShapeDtypeStruct((t, c), F32),
        scratch_shapes=[pltpu.VMEM((tb + HALO, cw), F32)],
        compiler_params=_cparams(("parallel", "arbitrary"), tb * cw * 16))(proj7, proj7, proj7, proj7, w_dw, b_dw)


def _conv_bwd(dproj7, du1, proj7, w_dw):
    _, t, c = proj7.shape
    tb = _conv_block(t)
    hb = tb // HALO
    nb = t // tb
    cw = CONV_COLS

    def body(dp_ref, du_ref, duh_ref, a_ref, g_ref, ah_ref, gh_ref, w_ref, out_ref, dw_ref, dext, uext, dwp):
        del dp_ref
        i = pl.program_id(1)
        dext[0:tb, :] = du_ref[...]
        dext[tb:tb + HALO, :] = jnp.where(i < nb - 1, duh_ref[...], 0.0)
        halo = ah_ref[...].astype(F32) * _sigmoid(gh_ref[...].astype(F32))
        uext[0:HALO, :] = jnp.where(i > 0, halo, 0.0)
        for r0 in range(0, tb, CONV_ROWS):
            a_c = a_ref[r0:r0 + CONV_ROWS, :].astype(F32)
            s_c = _sigmoid(g_ref[r0:r0 + CONV_ROWS, :].astype(F32))
            uext[HALO + r0:HALO + r0 + CONV_ROWS, :] = a_c * s_c
            acc = jnp.zeros((CONV_ROWS, cw), F32)
            for k in range(CONV_K):
                off = r0 + (CONV_K - 1) - k
                acc = acc + dext[off:off + CONV_ROWS, :] * w_ref[k:k + 1, :]
            out_ref[0, r0:r0 + CONV_ROWS, :] = (acc * s_c).astype(BF16)
            out_ref[1, r0:r0 + CONV_ROWS, :] = (acc * a_c * s_c * (1.0 - s_c)).astype(BF16)
        for k in range(CONV_K):
            acc = jnp.zeros((8, cw), F32)
            for r0 in range(0, tb, CONV_ROWS):
                off = r0 + HALO - (CONV_K - 1) + k
                prod = dext[r0:r0 + CONV_ROWS, :] * uext[off:off + CONV_ROWS, :]
                acc = acc + jnp.sum(prod.reshape(CONV_ROWS // 8, 8, cw), axis=0)
            dwp[k:k + 1, :] = jnp.sum(acc, axis=0, keepdims=True)
        dwp[CONV_K:32, :] = jnp.zeros((32 - CONV_K, cw), F32)

        @pl.when(i == 0)
        def _():
            dw_ref[...] = dwp[...]

        @pl.when(i > 0)
        def _():
            dw_ref[...] += dwp[...]

    blk = lambda seg: pl.BlockSpec((None, tb, cw), lambda j, i: (seg, i, j))
    halo = lambda seg: pl.BlockSpec((None, HALO, cw), lambda j, i: (seg, jnp.maximum(i * hb - 1, 0), j))
    est = tb * cw * 40
    return pl.pallas_call(
        body, name="conv_bwd", grid=(c // cw, nb),
        in_specs=[pl.BlockSpec(memory_space=pl.ANY),
                  pl.BlockSpec((tb, cw), lambda j, i: (i, j)),
                  pl.BlockSpec((HALO, cw), lambda j, i: (jnp.minimum((i + 1) * hb, nb * hb - 1), j)),
                  blk(4), blk(5), halo(4), halo(5), pl.BlockSpec((32, cw), lambda j, i: (0, j))],
        out_specs=[pl.BlockSpec((2, tb, cw), lambda j, i: (2, i, j)), pl.BlockSpec((32, cw), lambda j, i: (0, j))],
        out_shape=[jax.ShapeDtypeStruct(dproj7.shape, BF16), jax.ShapeDtypeStruct((32, c), F32)],
        scratch_shapes=[pltpu.VMEM((tb + HALO, cw), F32), pltpu.VMEM((tb + HALO, cw), F32), pltpu.VMEM((32, cw), F32)],
        input_output_aliases={0: 0},
        compiler_params=_cparams(("parallel", "arbitrary"), est))(dproj7, du1, du1, proj7, proj7, proj7, proj7, w_dw)


def _rel_index():
    j = np.arange(KB)
    key_minus_query = np.where(j < KB - CHUNK, j, j - KB) - QB
    return (np.clip(-key_minus_query, -MAX_REL, MAX_REL) + MAX_REL).astype(np.int32)


def _row_bits(shape):
    return lax.broadcasted_iota(jnp.int32, shape, 0)


def _bias_build(tvec):
    h = tvec.shape[0]

    def body(t_ref, o_ref):
        x = jnp.broadcast_to(t_ref[...], (QB, KB))
        row = _row_bits((QB, KB))
        for bit in range(QB.bit_length() - 1):
            x = jnp.where(((row >> bit) & 1) == 1, pltpu.roll(x, 1 << bit, 1), x)
        qc = row // CHUNK
        kc = lax.broadcasted_iota(jnp.int32, (QB, KB), 1) // CHUNK
        o_ref[...] = jnp.where((kc >= qc) & (kc <= qc + LEFT_CHUNKS), x, NEG_INF)

    return pl.pallas_call(
        body, name="bias_build", grid=(h,), in_specs=[pl.BlockSpec((None, 1, KB), lambda i: (i, 0, 0))],
        out_specs=pl.BlockSpec((None, QB, KB), lambda i: (i, 0, 0)), out_shape=jax.ShapeDtypeStruct((h, QB, KB), F32),
        compiler_params=_cparams(("parallel",), QB * KB * 16))(tvec)


def _bias_grad(ds):
    h = ds.shape[0]

    def body(d_ref, o_ref):
        x = d_ref[...]
        row = _row_bits((QB, KB))
        for bit in range(QB.bit_length() - 1):
            x = jnp.where(((row >> bit) & 1) == 1, pltpu.roll(x, KB - (1 << bit), 1), x)
        o_ref[...] = jnp.sum(x, axis=0, keepdims=True)

    return pl.pallas_call(
        body, name="bias_grad", grid=(h,), in_specs=[pl.BlockSpec((None, QB, KB), lambda i: (i, 0, 0))],
        out_specs=pl.BlockSpec((None, 1, KB), lambda i: (i, 0, 0)), out_shape=jax.ShapeDtypeStruct((h, 1, KB), F32),
        compiler_params=_cparams(("parallel",), QB * KB * 16))(ds)


def _attn_specs(a, nb, clamp):
    hpa = a // HEAD_DIM
    cur = (lambda i: jnp.minimum(i, nb - 1)) if clamp else (lambda i: i)
    prev = lambda i: jnp.maximum(cur(i) - 1, 0)
    q = pl.BlockSpec((None, QB, HEAD_DIM), lambda h, i: (0, cur(i), h))
    kp = pl.BlockSpec((None, QB, HEAD_DIM), lambda h, i: (1, prev(i), h))
    kc = pl.BlockSpec((None, QB, HEAD_DIM), lambda h, i: (1, cur(i), h))
    vp = pl.BlockSpec((None, QB, HEAD_DIM), lambda h, i: (2, prev(i), h))
    vc = pl.BlockSpec((None, QB, HEAD_DIM), lambda h, i: (2, cur(i), h))
    bias = pl.BlockSpec((None, QB, KB), lambda h, i: (h, 0, 0))
    del hpa
    return q, kp, kc, vp, vc, bias


def _scores(q, kp, kc, b_ref, first):
    scale = HEAD_DIM ** -0.5
    pen = jnp.where(first, NEG_INF, 0.0).astype(F32)
    sp = lax.dot_general(q, kp, NT, preferred_element_type=F32) * scale + b_ref[:, 0:QB] + pen
    sc = lax.dot_general(q, kc, NT, preferred_element_type=F32) * scale + b_ref[:, QB:KB]
    m = jnp.maximum(jnp.max(sp, axis=-1, keepdims=True), jnp.max(sc, axis=-1, keepdims=True))
    ep = jnp.exp(sp - m)
    ec = jnp.exp(sc - m)
    l = jnp.sum(ep, axis=-1, keepdims=True) + jnp.sum(ec, axis=-1, keepdims=True)
    return ep, ec, l


def _attn_fwd(proj7, bias):
    _, t, a = proj7.shape
    nb = t // QB
    h = a // HEAD_DIM

    def body(q_ref, kp_ref, kc_ref, vp_ref, vc_ref, b_ref, o_ref):
        i = pl.program_id(1)
        ep, ec, l = _scores(q_ref[...], kp_ref[...], kc_ref[...], b_ref, i == 0)
        o = (jnp.dot(ep.astype(BF16), vp_ref[...], preferred_element_type=F32)
             + jnp.dot(ec.astype(BF16), vc_ref[...], preferred_element_type=F32))
        o_ref[...] = o / l

    return pl.pallas_call(
        body, name="attn_fwd", grid=(h, nb), in_specs=list(_attn_specs(a, nb, False)),
        out_specs=pl.BlockSpec((QB, HEAD_DIM), lambda hh, i: (i, hh)),
        out_shape=jax.ShapeDtypeStruct((t, a), F32),
        compiler_params=_cparams(("parallel", "arbitrary"), QB * KB * 40))(proj7, proj7, proj7, proj7, proj7, bias)


def _attn_bwd(dproj7, proj7, do, bias):
    _, t, a = proj7.shape
    nb = t // QB
    h = a // HEAD_DIM
    scale = HEAD_DIM ** -0.5

    def body(dp_ref, q_ref, kp_ref, kc_ref, vp_ref, vc_ref, b_ref, do_ref, out_ref, ds_ref, dq_c, dk_c, dv_c):
        del dp_ref
        i = pl.program_id(1)

        @pl.when(i < nb)
        def _():
            q = q_ref[...]
            kp = kp_ref[...]
            kc = kc_ref[...]
            do = do_ref[...]
            ep, ec, l = _scores(q, kp, kc, b_ref, i == 0)
            inv = 1.0 / l
            pp = ep * inv
            pc = ec * inv
            dpp = lax.dot_general(do, vp_ref[...], NT, preferred_element_type=F32)
            dpc = lax.dot_general(do, vc_ref[...], NT, preferred_element_type=F32)
            delta = jnp.sum(pp * dpp, axis=-1, keepdims=True) + jnp.sum(pc * dpc, axis=-1, keepdims=True)
            dsp = pp * (dpp - delta)
            dsc = pc * (dpc - delta)

            @pl.when(i == 0)
            def _():
                ds_ref[:, 0:QB] = dsp
                ds_ref[:, QB:KB] = dsc

            @pl.when(i > 0)
            def _():
                ds_ref[:, 0:QB] += dsp
                ds_ref[:, QB:KB] += dsc

            dspb = (dsp * scale).astype(BF16)
            dscb = (dsc * scale).astype(BF16)
            dq = jnp.dot(dspb, kp, preferred_element_type=F32) + jnp.dot(dscb, kc, preferred_element_type=F32)
            dkp = lax.dot_general(dspb, q, TN, preferred_element_type=F32)
            dkc = lax.dot_general(dscb, q, TN, preferred_element_type=F32)
            dvp = lax.dot_general(pp.astype(BF16), do, TN, preferred_element_type=F32)
            dvc = lax.dot_general(pc.astype(BF16), do, TN, preferred_element_type=F32)

            @pl.when(i > 0)
            def _():
                out_ref[0] = dq_c[...].astype(BF16)
                out_ref[1] = (dk_c[...] + dkp).astype(BF16)
                out_ref[2] = (dv_c[...] + dvp).astype(BF16)

            dq_c[...] = dq
            dk_c[...] = dkc
            dv_c[...] = dvc

        @pl.when(i == nb)
        def _():
            out_ref[0] = dq_c[...].astype(BF16)
            out_ref[1] = dk_c[...].astype(BF16)
            out_ref[2] = dv_c[...].astype(BF16)

    q, kp, kc, vp, vc, bspec = _attn_specs(a, nb, True)
    return pl.pallas_call(
        body, name="attn_bwd", grid=(h, nb + 1),
        in_specs=[pl.BlockSpec(memory_space=pl.ANY), q, kp, kc, vp, vc, bspec,
                  pl.BlockSpec((None, QB, HEAD_DIM), lambda hh, i: (0, jnp.minimum(i, nb - 1), hh))],
        out_specs=[pl.BlockSpec((3, QB, HEAD_DIM), lambda hh, i: (0, jnp.maximum(i - 1, 0), hh)),
                   pl.BlockSpec((None, QB, KB), lambda hh, i: (hh, 0, 0))],
        out_shape=[jax.ShapeDtypeStruct(dproj7.shape, BF16), jax.ShapeDtypeStruct((h, QB, KB), F32)],
        scratch_shapes=[pltpu.VMEM((QB, HEAD_DIM), F32)] * 3,
        input_output_aliases={0: 0},
        compiler_params=_cparams(("parallel", "arbitrary"), QB * KB * 60))(dproj7, proj7, proj7, proj7, proj7, proj7, bias, do)


def _tile(n, pref):
    t = min(n, pref)
    while n % t:
        t //= 2
    return t


def _mm(name, a, b, dn, grid, a_spec, b_spec, acc_shape, out_shape, out_specs, epilogue, extra=(), extra_specs=(), est=0):
    nk = grid[2]
    n_extra = len(extra)

    def body(*refs):
        a_ref, b_ref = refs[0], refs[1]
        ex = refs[2:2 + n_extra]
        outs = refs[2 + n_extra:-1]
        acc = refs[-1]
        k = pl.program_id(2)
        prod = lax.dot_general(a_ref[...].astype(BF16), b_ref[...].astype(BF16), dn, preferred_element_type=F32)

        @pl.when(k == 0)
        def _():
            acc[...] = prod

        @pl.when(k > 0)
        def _():
            acc[...] += prod

        @pl.when(k == nk - 1)
        def _():
            epilogue(acc[...], ex, outs)

    return pl.pallas_call(
        body, name=name, grid=grid, in_specs=[a_spec, b_spec, *extra_specs], out_specs=out_specs, out_shape=out_shape,
        scratch_shapes=[pltpu.VMEM(acc_shape, F32)],
        compiler_params=_cparams(("parallel", "parallel", "arbitrary"), est))(a, b, *extra)


def _store(dtype):
    def ep(acc, ex, outs):
        outs[0][...] = acc.astype(dtype)
    return ep


def _mm_est(tm, tn, tk, out_bytes):
    return 2 * (tm * tk * 4 + tk * tn * 2) + tm * tn * 4 + 2 * tm * tn * out_bytes


def _mm_nn(name, a, b, out_dtype, bias=None, resid=None):
    m, k = a.shape
    n = b.shape[1]
    tm, tn, tk = _tile(m, 1024), _tile(n, 1024), _tile(k, 1024)
    extra, especs = [], []
    if bias is not None:
        extra.append(bias)
        especs.append(pl.BlockSpec((1, tn), lambda i, j, kk: (0, j)))
    if resid is not None:
        extra.append(resid)
        especs.append(pl.BlockSpec((tm, tn), lambda i, j, kk: (i, j)))

    def ep(acc, ex, outs):
        for r in ex:
            acc = acc + r[...]
        outs[0][...] = acc.astype(out_dtype)

    return _mm(name, a, b, NN, (m // tm, n // tn, k // tk),
               pl.BlockSpec((tm, tk), lambda i, j, kk: (i, kk)), pl.BlockSpec((tk, tn), lambda i, j, kk: (kk, j)),
               (tm, tn), jax.ShapeDtypeStruct((m, n), out_dtype), pl.BlockSpec((tm, tn), lambda i, j, kk: (i, j)),
               ep, extra, especs, _mm_est(tm, tn, tk, 12))


def _mm_nt(name, a, b, out_dtype):
    m, k = a.shape
    n = b.shape[0]
    tm, tn, tk = _tile(m, 1024), _tile(n, 1024), _tile(k, 1024)
    return _mm(name, a, b, NT, (m // tm, n // tn, k // tk),
               pl.BlockSpec((tm, tk), lambda i, j, kk: (i, kk)), pl.BlockSpec((tn, tk), lambda i, j, kk: (j, kk)),
               (tm, tn), jax.ShapeDtypeStruct((m, n), out_dtype), pl.BlockSpec((tm, tn), lambda i, j, kk: (i, j)),
               _store(out_dtype), est=_mm_est(tm, tn, tk, 4))


def _mm_tn(name, a, b, out_dtype):
    k, m = a.shape
    n = b.shape[1]
    tm, tn, tk = _tile(m, 1024), _tile(n, 1024), _tile(k, 1024)
    return _mm(name, a, b, TN, (m // tm, n // tn, k // tk),
               pl.BlockSpec((tk, tm), lambda i, j, kk: (kk, i)), pl.BlockSpec((tk, tn), lambda i, j, kk: (kk, j)),
               (tm, tn), jax.ShapeDtypeStruct((m, n), out_dtype), pl.BlockSpec((tm, tn), lambda i, j, kk: (i, j)),
               _store(out_dtype), est=_mm_est(tm, tn, tk, 4))


def _proj_fwd(xn, w_in, seg):
    t, d = xn.shape
    ncol = w_in.shape[1]
    tm, tn, tk = _tile(t, 1024), _tile(seg, 1024), _tile(d, 1024)
    per = seg // tn
    return _mm("proj_fwd", xn, w_in, NN, (t // tm, ncol // tn, d // tk),
               pl.BlockSpec((tm, tk), lambda i, j, kk: (i, kk)), pl.BlockSpec((tk, tn), lambda i, j, kk: (kk, j)),
               (tm, tn), jax.ShapeDtypeStruct((ncol // seg, t, seg), BF16),
               pl.BlockSpec((None, tm, tn), lambda i, j, kk: (j // per, i, j % per)),
               _store(BF16), est=_mm_est(tm, tn, tk, 2))


def _proj_bwd_x(dproj7, w_in):
    nseg, t, seg = dproj7.shape
    d = w_in.shape[0]
    tm, tn, tk = _tile(t, 1024), _tile(d, 1024), _tile(seg, 1024)
    per = seg // tk
    return _mm("proj_bwd_x", dproj7, w_in, NT, (t // tm, d // tn, nseg * per),
               pl.BlockSpec((None, tm, tk), lambda i, j, kk: (kk // per, i, kk % per)),
               pl.BlockSpec((tn, tk), lambda i, j, kk: (j, kk)),
               (tm, tn), jax.ShapeDtypeStruct((t, d), F32), pl.BlockSpec((tm, tn), lambda i, j, kk: (i, j)),
               _store(F32), est=_mm_est(tm, tn, tk, 4))


def _proj_bwd_w(xn, dproj7):
    nseg, t, seg = dproj7.shape
    d = xn.shape[1]
    tm, tn, tk = _tile(d, 1024), _tile(seg, 1024), _tile(t, 1024)
    per = seg // tn
    return _mm("proj_bwd_w", xn, dproj7, TN, (d // tm, nseg * per, t // tk),
               pl.BlockSpec((tk, tm), lambda i, j, kk: (kk, i)),
               pl.BlockSpec((None, tk, tn), lambda i, j, kk: (j // per, kk, j % per)),
               (tm, tn), jax.ShapeDtypeStruct((d, nseg * seg), BF16), pl.BlockSpec((tm, tn), lambda i, j, kk: (i, j)),
               _store(BF16), est=_mm_est(tm, tn, tk, 2))


def _gate_fwd(hn, w_g, b_g, h1, p, w_ple):
    t, d = hn.shape
    pd = p.shape[1]
    tm, tn, tk = _tile(t, 1024), _tile(d, 1024), _tile(d, 1024)

    def ep(acc, ex, outs):
        b_ref, h1_ref, p_ref, wp_ref = ex
        gate = _sigmoid(acc + b_ref[...])
        pe = jnp.dot(p_ref[...].astype(BF16), wp_ref[...], preferred_element_type=F32)
        outs[0][...] = h1_ref[...] + gate * pe
        outs[1][...] = gate.astype(BF16)
        outs[2][...] = pe.astype(BF16)

    tile = pl.BlockSpec((tm, tn), lambda i, j, kk: (i, j))
    return _mm("gate_fwd", hn, w_g, NN, (t // tm, d // tn, d // tk),
               pl.BlockSpec((tm, tk), lambda i, j, kk: (i, kk)), pl.BlockSpec((tk, tn), lambda i, j, kk: (kk, j)),
               (tm, tn),
               [jax.ShapeDtypeStruct((t, d), F32), jax.ShapeDtypeStruct((t, d), BF16), jax.ShapeDtypeStruct((t, d), BF16)],
               [tile, tile, tile], ep, [b_g, h1, p, w_ple],
               [pl.BlockSpec((1, tn), lambda i, j, kk: (0, j)), tile, pl.BlockSpec((tm, pd), lambda i, j, kk: (i, 0)),
                pl.BlockSpec((pd, tn), lambda i, j, kk: (0, j))],
               _mm_est(tm, tn, tk, 16) + 2 * (tm * pd * 4 + pd * tn * 2))


def _coords():
    return lax.axis_index("x"), lax.axis_index("y"), lax.axis_index("c")


def _row_slice(rows):
    return lambda ref, o: ref.at[pl.ds(pl.multiple_of(o * rows, rows), rows), :]


def _col_slice(cols):
    return lambda ref, o: ref.at[:, pl.ds(pl.multiple_of(o * cols, cols), cols)]


def _all_gather(shards, slicers, full_shapes):
    n = len(shards)

    def body(*refs):
        srcs, outs = refs[:n], refs[n:2 * n]
        send_sems, recv_sems, local_sems = refs[2 * n:]
        x, y, c = _coords()
        me, sibling = (x, y, c), (x, y, 1 - c)
        chips = [(1 - x, y), (x, 1 - y), (1 - x, 1 - y)]

        def copy(w, k, block, to, src=None):
            dst = slicers[w](outs[w], 4 * block[0] + 2 * block[1] + block[2])
            return pltpu.make_async_remote_copy(
                src_ref=dst if src is None else src, dst_ref=dst, send_sem=send_sems.at[w, k], recv_sem=recv_sems.at[w, k],
                device_id=to, device_id_type=MESH)

        mine = [pltpu.make_async_copy(srcs[w], slicers[w](outs[w], 4 * x + 2 * y + c), local_sems.at[w]) for w in range(n)]
        for cp in mine:
            cp.start()
        first = []
        for w in range(n):
            first.append(copy(w, 0, me, sibling, src=srcs[w]))
            first += [copy(w, 1 + j, me, (*chip, c), src=srcs[w]) for j, chip in enumerate(chips)]
        for cp in first:
            cp.start()
        passed = []
        for j, chip in enumerate(chips):
            for w in range(n):
                copy(w, 1 + j, (*chip, c), me).wait_recv()
                fwd = copy(w, 4 + j, (*chip, c), sibling)
                fwd.start()
                passed.append(fwd)
        for w in range(n):
            copy(w, 0, sibling, me).wait_recv()
            for j, chip in enumerate(chips):
                copy(w, 4 + j, (*chip, 1 - c), me).wait_recv()
        for cp in first + passed:
            cp.wait_send()
        for cp in mine:
            cp.wait()

    hbm = pl.BlockSpec(memory_space=pl.ANY)
    return pl.pallas_call(
        body, name="all_gather_weights",
        out_shape=[jax.ShapeDtypeStruct(s, a.dtype) for s, a in zip(full_shapes, shards)],
        in_specs=[hbm] * n, out_specs=[hbm] * n,
        scratch_shapes=[pltpu.SemaphoreType.DMA((n, 7)), pltpu.SemaphoreType.DMA((n, 7)), pltpu.SemaphoreType.DMA((n,))],
        compiler_params=pltpu.CompilerParams(has_side_effects=True))(*shards)


def _pair_exchange(grads, slicers, shard_shapes):
    n = len(grads)

    def body(*refs):
        srcs, outs = refs[:n], refs[n:2 * n]
        send_sems, recv_sems = refs[2 * n:]
        x, y, c = _coords()
        sibling = (x, y, 1 - c)
        copies = []
        for w in range(n):
            for j in range(4):
                copies.append(pltpu.make_async_remote_copy(
                    src_ref=slicers[w](srcs[w], 2 * j + (1 - c)), dst_ref=outs[w].at[j],
                    send_sem=send_sems.at[w, j], recv_sem=recv_sems.at[w, j], device_id=sibling, device_id_type=MESH))
        for cp in copies:
            cp.start()
        for cp in copies:
            cp.wait_recv()
        for cp in copies:
            cp.wait_send()

    hbm = pl.BlockSpec(memory_space=pl.ANY)
    return pl.pallas_call(
        body, name="grad_pair_exchange",
        out_shape=[jax.ShapeDtypeStruct((4, *s), g.dtype) for s, g in zip(shard_shapes, grads)],
        in_specs=[hbm] * n, out_specs=[hbm] * n,
        scratch_shapes=[pltpu.SemaphoreType.DMA((n, 4)), pltpu.SemaphoreType.DMA((n, 4))],
        compiler_params=pltpu.CompilerParams(has_side_effects=True))(*grads)


def _pair_add(name, grad, recv, col_sharded, cidx):
    _, r, cc = recv.shape
    tr = _row_block(r, cc * 6, budget=8 << 20)
    nr = r // tr
    if col_sharded:
        g_spec = pl.BlockSpec((tr, cc), lambda j, i, s: (i, 2 * j + s[0]))
    else:
        g_spec = pl.BlockSpec((tr, cc), lambda j, i, s: ((2 * j + s[0]) * nr + i, 0))

    def body(s_ref, g_ref, r_ref, o_ref):
        del s_ref
        o_ref[...] = (g_ref[...].astype(F32) + r_ref[...].astype(F32)).astype(BF16)

    return pl.pallas_call(
        body, name=name,
        grid_spec=pltpu.PrefetchScalarGridSpec(
            num_scalar_prefetch=1, grid=(4, nr),
            in_specs=[g_spec, pl.BlockSpec((None, tr, cc), lambda j, i, s: (j, i, 0))],
            out_specs=pl.BlockSpec((None, tr, cc), lambda j, i, s: (j, i, 0))),
        out_shape=jax.ShapeDtypeStruct(recv.shape, BF16),
        compiler_params=_cparams(("parallel", "parallel"), tr * cc * 12))(cidx, grad, recv)


def _chip_exchange(pairsums):
    n = len(pairsums)

    def body(*refs):
        srcs, outs = refs[:n], refs[n:2 * n]
        send_sems, recv_sems = refs[2 * n:]
        x, y, c = _coords()
        copies = []
        for w in range(n):
            for r in (1, 2, 3):
                px, py = x ^ (r >> 1), y ^ (r & 1)
                copies.append(pltpu.make_async_remote_copy(
                    src_ref=srcs[w].at[2 * px + py], dst_ref=outs[w].at[r - 1],
                    send_sem=send_sems.at[w, r - 1], recv_sem=recv_sems.at[w, r - 1], device_id=(px, py, c), device_id_type=MESH))
        for cp in copies:
            cp.start()
        for cp in copies:
            cp.wait_recv()
        for cp in copies:
            cp.wait_send()

    hbm = pl.BlockSpec(memory_space=pl.ANY)
    return pl.pallas_call(
        body, name="grad_chip_exchange",
        out_shape=[jax.ShapeDtypeStruct((3, *p.shape[1:]), p.dtype) for p in pairsums],
        in_specs=[hbm] * n, out_specs=[hbm] * n,
        scratch_shapes=[pltpu.SemaphoreType.DMA((n, 3)), pltpu.SemaphoreType.DMA((n, 3))],
        compiler_params=pltpu.CompilerParams(has_side_effects=True))(*pairsums)


def _all_reduce_small(vec):
    r = vec.shape[0]

    def body(v_ref, o_ref, gath, send_sems, recv_sems):
        x, y, c = _coords()
        me = 4 * x + 2 * y + c
        copies = []
        for rel in range(1, N_DEV):
            peer = (x ^ (rel >> 2), y ^ ((rel >> 1) & 1), c ^ (rel & 1))
            copies.append(pltpu.make_async_remote_copy(
                src_ref=v_ref, dst_ref=gath.at[me], send_sem=send_sems.at[rel - 1], recv_sem=recv_sems.at[rel - 1],
                device_id=peer, device_id_type=MESH))
        for cp in copies:
            cp.start()
        gath[me] = v_ref[...]
        for rel in range(1, N_DEV):
            src = 4 * (x ^ (rel >> 2)) + 2 * (y ^ ((rel >> 1) & 1)) + (c ^ (rel & 1))
            pltpu.make_async_remote_copy(
                src_ref=v_ref, dst_ref=gath.at[src], send_sem=send_sems.at[rel - 1], recv_sem=recv_sems.at[rel - 1],
                device_id=(x, y, c), device_id_type=MESH).wait_recv()
        for cp in copies:
            cp.wait_send()
        acc = gath[0]
        for d in range(1, N_DEV):
            acc = acc + gath[d]
        o_ref[...] = acc

    vm = pl.BlockSpec(memory_space=pltpu.VMEM)
    return pl.pallas_call(
        body, name="all_reduce_small", out_shape=jax.ShapeDtypeStruct(vec.shape, F32), in_specs=[vm], out_specs=vm,
        scratch_shapes=[pltpu.VMEM((N_DEV, r, LANES), F32), pltpu.SemaphoreType.DMA((N_DEV - 1,)),
                        pltpu.SemaphoreType.DMA((N_DEV - 1,))],
        compiler_params=pltpu.CompilerParams(has_side_effects=True, vmem_limit_bytes=int(min(VMEM_CAP, 32 * r * LANES * 4 + (16 << 20)))))(vec)


def _adamw_math(w, g, m, v):
    m = ADAM_B1 * m + (1.0 - ADAM_B1) * g
    v = ADAM_B2 * v + (1.0 - ADAM_B2) * (g * g)
    m_hat = m / (1.0 - ADAM_B1 ** ADAM_STEP)
    v_hat = v / (1.0 - ADAM_B2 ** ADAM_STEP)
    delta = -ADAM_LR * (m_hat / (jnp.sqrt(v_hat) + ADAM_EPS) + ADAM_WD * w)
    return delta, m, v


def _adamw_big(name, pairsum, recv, w, m, v, chip_idx):
    r, cc = w.shape
    tr = _row_block(r, cc * 36, budget=16 << 20)

    def body(s_ref, p_ref, r_ref, w_ref, m_ref, v_ref, g_out, d_out, m_out, v_out):
        del s_ref
        g = p_ref[...].astype(F32)
        for k in range(3):
            g = g + r_ref[k].astype(F32)
        delta, m2, v2 = _adamw_math(w_ref[...], g, m_ref[...], v_ref[...])
        g_out[...] = g
        d_out[...] = delta
        m_out[...] = m2
        v_out[...] = v2

    tile = pl.BlockSpec((tr, cc), lambda i, s: (i, 0))
    sds = jax.ShapeDtypeStruct((r, cc), F32)
    return pl.pallas_call(
        body, name=name,
        grid_spec=pltpu.PrefetchScalarGridSpec(
            num_scalar_prefetch=1, grid=(r // tr,),
            in_specs=[pl.BlockSpec((None, tr, cc), lambda i, s: (s[0], i, 0)), pl.BlockSpec((3, tr, cc), lambda i, s: (0, i, 0)),
                      tile, tile, tile],
            out_specs=[tile, tile, tile, tile]),
        out_shape=[sds, sds, sds, sds],
        compiler_params=_cparams(("parallel",), tr * cc * 72))(chip_idx, pairsum, recv, w, m, v)


def _adamw_small(g, w, m, v):
    r = g.shape[0]

    def body(g_ref, w_ref, m_ref, v_ref, d_out, m_out, v_out):
        delta, m2, v2 = _adamw_math(w_ref[...], g_ref[...], m_ref[...], v_ref[...])
        d_out[...] = delta
        m_out[...] = m2
        v_out[...] = v2

    vm = pl.BlockSpec(memory_space=pltpu.VMEM)
    sds = jax.ShapeDtypeStruct((r, LANES), F32)
    return pl.pallas_call(body, name="adamw_small", out_shape=[sds, sds, sds], in_specs=[vm] * 4, out_specs=[vm] * 3,
                          compiler_params=_cparams(None, r * LANES * 28))(g, w, m, v)


def _pack(parts, rows):
    flat = jnp.concatenate([q.reshape(-1).astype(F32) for q in parts])
    return jnp.pad(flat, (0, rows * LANES - flat.shape[0])).reshape(rows, LANES)


def _unpack(packed, shapes):
    flat = packed.reshape(-1)
    out, off = [], 0
    for s in shapes:
        n = int(np.prod(s))
        out.append(flat[off:off + n].reshape(s))
        off += n
    return out


def kernel(x, p, norm_in_g, w_in, rel_table, w_dw, b_dw, conv_ln_g, conv_ln_b, w_pw, b_pw, attn_out_g, conv_out_g, w_out, ple_norm_g, w_ple_gate, b_ple_gate, w_ple, final_g, loss_target, m_norm_in_g, m_w_in, m_rel_table, m_w_dw, m_b_dw, m_conv_ln_g, m_conv_ln_b, m_w_pw, m_b_pw, m_attn_out_g, m_conv_out_g, m_w_out, m_ple_norm_g, m_w_ple_gate, m_b_ple_gate, m_w_ple, m_final_g, v_norm_in_g, v_w_in, v_rel_table, v_w_dw, v_b_dw, v_conv_ln_g, v_conv_ln_b, v_w_pw, v_b_pw, v_attn_out_g, v_conv_out_g, v_w_out, v_ple_norm_g, v_w_ple_gate, v_b_ple_gate, v_w_ple, v_final_g):
    t, d = x.shape[1], x.shape[2]
    a = d // 2
    ncol = 7 * a
    ns = ncol // N_DEV
    pd = p.shape[-1]
    heads = a // HEAD_DIM
    assert x.shape[0] == 1 and t % QB == 0 and ns % LANES == 0 and w_in.shape == (1, d, ns)
    cx, cy, cc_ = _coords()
    me = 4 * cx + 2 * cy + cc_
    core_idx = jnp.reshape(cc_, (1,)).astype(jnp.int32)
    chip_idx = jnp.reshape(2 * cx + cy, (1,)).astype(jnp.int32)

    x2, tgt, p2 = x[0], loss_target[0], p[0, 0]
    final_g2 = final_g.reshape(1, d)

    w_dw_pad = jnp.pad(w_dw[0], ((0, 32 - CONV_K), (0, 0)))
    shards = [_cast_bf16(w_in[0], "cast_w_in"), _cast_bf16(w_out[0], "cast_w_out"), _cast_bf16(w_ple_gate[0], "cast_w_gate"),
              _cast_bf16(w_pw[0], "cast_w_pw"), _cast_bf16(w_ple[0], "cast_w_ple"), w_dw_pad]
    slicers = [_col_slice(ns), _row_slice(d // N_DEV), _row_slice(d // N_DEV), _row_slice(a // N_DEV),
               _col_slice(d // N_DEV), _col_slice(a // N_DEV)]
    full_shapes = [(d, ncol), (d, d), (d, d), (a, a), (pd, d), (32, a)]
    wf_in, wf_out, wf_gate, wf_pw, wf_ple, wf_dw = _all_gather(shards, slicers, full_shapes)

    xn = _rms_fwd(x2, norm_in_g, "rms_in_fwd")
    proj7 = _proj_fwd(xn, wf_in, a)
    tvec = rel_table[0][:, _rel_index()].reshape(heads, 1, KB)
    bias = _bias_build(tvec)
    o = _attn_fwd(proj7, bias)
    u1 = _conv_fwd(proj7, wf_dw, b_dw)
    u3 = _ln_silu_fwd(u1, conv_ln_g, conv_ln_b)
    yc = _mm_nn("pw_fwd", u3, wf_pw, F32, bias=b_pw)
    gains = jnp.stack([attn_out_g, conv_out_g])
    ycat = _post_fwd(o, yc, proj7, gains)
    h1 = _mm_nn("out_fwd", ycat, wf_out, F32, resid=x2)
    hn = _rms_fwd(h1, ple_norm_g, "rms_ple_fwd")
    h2, gate, pe = _gate_fwd(hn, wf_gate, b_ple_gate, h1, p2, wf_ple)

    dh2, dpe, dgl, loss_v, d_final_g, d_b_gate = _final(h2, tgt, gate, pe, final_g2)
    dhn = _mm_nt("gate_bwd_x", dgl, wf_gate, F32)
    dw_gate = _mm_tn("gate_bwd_w", hn, dgl, BF16)
    dw_ple = _mm_tn("ple_bwd_w", p2, dpe, BF16)
    dh1, dh1b, d_ple_norm_g = _rms_bwd(dhn, h1, ple_norm_g, dh2, "rms_ple_bwd", True)
    dycat = _mm_nt("out_bwd_x", dh1b, wf_out, F32)
    dw_out = _mm_tn("out_bwd_w", ycat, dh1b, BF16)
    dproj7, dxin, d_gains, cs_dxin = _post_bwd(dycat, o, yc, proj7, gains)
    dproj7, ds = _attn_bwd(dproj7, proj7, dxin, bias)
    dyc = dxin[1]
    du3 = _mm_nt("pw_bwd_x", dyc, wf_pw, F32)
    dw_pw = _mm_tn("pw_bwd_w", u3, dyc, BF16)
    du1, d_ln_g, d_ln_b, d_b_dw = _ln_silu_bwd(du3, u1, conv_ln_g, conv_ln_b)
    dproj7, d_w_dw = _conv_bwd(dproj7, du1, proj7, wf_dw)
    dxn = _proj_bwd_x(dproj7, wf_in)
    dw_in = _proj_bwd_w(xn, dproj7)
    grad_x, d_norm_in_g = _rms_bwd(dxn, x2, norm_in_g, dh1, "rms_in_bwd", False)

    dcol = _bias_grad(ds).reshape(heads, KB)
    onehot = jnp.asarray(_rel_index()[:, None] == np.arange(2 * MAX_REL + 1)[None, :], F32)
    d_rel = jnp.einsum("hj,jr->hr", dcol, onehot, precision=lax.Precision.HIGHEST)

    big = [("w_in", dw_in, True, ns, w_in, m_w_in, v_w_in), ("w_out", dw_out, False, d // N_DEV, w_out, m_w_out, v_w_out),
           ("w_gate", dw_gate, False, d // N_DEV, w_ple_gate, m_w_ple_gate, v_w_ple_gate),
           ("w_pw", dw_pw, False, a // N_DEV, w_pw, m_w_pw, v_w_pw), ("w_ple", dw_ple, True, d // N_DEV, w_ple, m_w_ple, v_w_ple)]
    g_slicers = [(_col_slice if col else _row_slice)(sz) for _, _, col, sz, *_ in big]
    shard_shapes = [wt.shape[1:] for _, _, _, _, wt, _, _ in big]
    recv1 = _pair_exchange([g for _, g, *_ in big], g_slicers, shard_shapes)
    pairs = [_pair_add("pair_add_" + nm, g, r1, col, core_idx) for (nm, g, col, *_), r1 in zip(big, recv1)]
    recv2 = _chip_exchange(pairs)
    big_out = {}
    for (nm, _, _, _, wt, mt, vt), ps, r2 in zip(big, pairs, recv2):
        g_, d_, m_, v_ = _adamw_big("adamw_" + nm, ps, r2, wt[0], mt[0], vt[0], chip_idx)
        big_out[nm] = tuple(q[None] for q in (g_, d_, m_, v_))

    small = [("norm_in_g", d_norm_in_g, norm_in_g, m_norm_in_g, v_norm_in_g),
             ("rel_table", d_rel, rel_table, m_rel_table, v_rel_table),
             ("b_dw", d_b_dw, b_dw, m_b_dw, v_b_dw),
             ("conv_ln_g", d_ln_g, conv_ln_g, m_conv_ln_g, v_conv_ln_g),
             ("conv_ln_b", d_ln_b, conv_ln_b, m_conv_ln_b, v_conv_ln_b),
             ("b_pw", cs_dxin[1], b_pw, m_b_pw, v_b_pw),
             ("attn_out_g", d_gains[0], attn_out_g, m_attn_out_g, v_attn_out_g),
             ("conv_out_g", d_gains[1], conv_out_g, m_conv_out_g, v_conv_out_g),
             ("ple_norm_g", d_ple_norm_g, ple_norm_g, m_ple_norm_g, v_ple_norm_g),
             ("b_ple_gate", d_b_gate, b_ple_gate, m_b_ple_gate, v_b_ple_gate),
             ("final_g", d_final_g, final_g, m_final_g, v_final_g)]
    g_parts = [g for _, g, *_ in small] + [d_w_dw[:CONV_K], loss_v[0, :1]]
    n_small = sum(int(np.prod(q.shape)) for q in g_parts)
    rows = -(-n_small // (8 * LANES)) * 8
    g_all = _all_reduce_small(_pack(g_parts, rows))
    shapes = [wt.shape for _, _, wt, _, _ in small] + [(CONV_K, a), (1,)]
    g_un = _unpack(g_all, shapes)
    loss = g_un[-1][0]
    g_w_dw = lax.dynamic_slice_in_dim(g_un[-2], me * (a // N_DEV), a // N_DEV, axis=1)[None]
    g_small = g_un[:len(small)] + [g_w_dw]
    rows2 = -(-(n_small - 1) // (8 * LANES)) * 8
    pk = lambda parts: _pack(parts, rows2)
    d_pk, m_pk, v_pk = _adamw_small(pk(g_small), pk([wt for _, _, wt, _, _ in small] + [w_dw]),
                                    pk([mt for _, _, _, mt, _ in small] + [m_w_dw]), pk([vt for _, _, _, _, vt in small] + [v_w_dw]))
    shapes2 = [wt.shape for _, _, wt, _, _ in small] + [w_dw.shape]
    names = [nm for nm, *_ in small] + ["w_dw"]
    small_out = {nm: (g, dd, mm, vv) for nm, g, dd, mm, vv in
                 zip(names, g_small, _unpack(d_pk, shapes2), _unpack(m_pk, shapes2), _unpack(v_pk, shapes2))}

    order = ["norm_in_g", "w_in", "rel_table", "w_dw", "b_dw", "conv_ln_g", "conv_ln_b", "w_pw", "b_pw", "attn_out_g",
             "conv_out_g", "w_out", "ple_norm_g", "w_ple_gate", "b_ple_gate", "w_ple", "final_g"]
    alias = {"w_ple_gate": "w_gate"}
    res = {nm: (big_out[alias.get(nm, nm)] if alias.get(nm, nm) in big_out else small_out[nm]) for nm in order}
    outs = [loss, grad_x[None]]
    for kind in range(4):
        outs += [res[nm][kind].reshape(w_shape) for nm, w_shape in zip(order, [
            norm_in_g.shape, w_in.shape, rel_table.shape, w_dw.shape, b_dw.shape, conv_ln_g.shape, conv_ln_b.shape, w_pw.shape,
            b_pw.shape, attn_out_g.shape, conv_out_g.shape, w_out.shape, ple_norm_g.shape, w_ple_gate.shape, b_ple_gate.shape,
            w_ple.shape, final_g.shape])]
    return tuple(outs)
```

```python
import functools

import numpy as np
import jax
import jax.numpy as jnp
from jax import lax
from jax.experimental import pallas as pl
from jax.experimental.pallas import tpu as pltpu

F32 = jnp.float32
BF16 = jnp.bfloat16
MESH = pl.DeviceIdType.MESH

CHUNK = 64
LEFT_CHUNKS = 8
HEAD_DIM = 128
MAX_REL = 256
CONV_K = 31
EPS = 1e-6
NEG_INF = -1e30
ADAM_LR = 0.001
ADAM_B1 = 0.9
ADAM_B2 = 0.999
ADAM_EPS = 1e-08
ADAM_WD = 0.01
ADAM_STEP = 10

N_DEV = 8
QB = CHUNK * LEFT_CHUNKS
KB = 2 * QB
HALO = 32
CONV_COLS = 256
CONV_ROWS = 64
LANES = 128
VMEM_CAP = 60 * 1024 * 1024
NT = (((1,), (1,)), ((), ()))
TN = (((0,), (0,)), ((), ()))
NN = (((1,), (0,)), ((), ()))


def _cparams(sem, est_bytes):
    return pltpu.CompilerParams(dimension_semantics=sem, vmem_limit_bytes=int(min(VMEM_CAP, max(32 << 20, 2 * est_bytes))))


def _nbytes(shape, dtype):
    return int(np.prod(shape)) * jnp.dtype(dtype).itemsize


def _row_block(rows, bytes_per_row, budget=12 << 20):
    tb = rows
    while tb > 16 and (tb * bytes_per_row * 2 > budget or rows % tb):
        tb //= 2
    assert rows % tb == 0
    return tb


def _sigmoid(x):
    return 1.0 / (1.0 + jnp.exp(-x))


def _cast_bf16(w, name):
    r, c = w.shape
    tb = _row_block(r, c * 6)

    def body(w_ref, o_ref):
        o_ref[...] = w_ref[...].astype(BF16)

    return pl.pallas_call(
        body, name=name, grid=(r // tb,),
        in_specs=[pl.BlockSpec((tb, c), lambda i: (i, 0))], out_specs=pl.BlockSpec((tb, c), lambda i: (i, 0)),
        out_shape=jax.ShapeDtypeStruct((r, c), BF16), compiler_params=_cparams(("parallel",), tb * c * 12))(w)


def _rms_fwd(x, g, name):
    t, d = x.shape
    tb = _row_block(t, d * 6)

    def body(x_ref, g_ref, o_ref):
        xv = x_ref[...]
        r = lax.rsqrt(jnp.mean(xv * xv, axis=-1, keepdims=True) + EPS)
        o_ref[...] = (xv * r * g_ref[...]).astype(BF16)

    return pl.pallas_call(
        body, name=name, grid=(t // tb,),
        in_specs=[pl.BlockSpec((tb, d), lambda i: (i, 0)), pl.BlockSpec((1, d), lambda i: (0, 0))],
        out_specs=pl.BlockSpec((tb, d), lambda i: (i, 0)),
        out_shape=jax.ShapeDtypeStruct((t, d), BF16), compiler_params=_cparams(("parallel",), tb * d * 12))(x, g)


def _rms_bwd(dy, x, g, resid, name, with_bf16):
    t, d = x.shape
    tb = _row_block(t, d * 18)

    def body(dy_ref, x_ref, g_ref, res_ref, *outs):
        i = pl.program_id(0)
        xv = x_ref[...]
        r = lax.rsqrt(jnp.mean(xv * xv, axis=-1, keepdims=True) + EPS)
        n = xv * r
        dyv = dy_ref[...].astype(F32)
        dn = dyv * g_ref[...]
        dx = res_ref[...] + r * (dn - n * jnp.mean(dn * n, axis=-1, keepdims=True))
        outs[0][...] = dx
        if with_bf16:
            outs[1][...] = dx.astype(BF16)
        dg_ref = outs[-1]
        part = jnp.sum(dyv * n, axis=0, keepdims=True)

        @pl.when(i == 0)
        def _():
            dg_ref[...] = part

        @pl.when(i > 0)
        def _():
            dg_ref[...] += part

    row = pl.BlockSpec((tb, d), lambda i: (i, 0))
    vec = pl.BlockSpec((1, d), lambda i: (0, 0))
    out_shape = [jax.ShapeDtypeStruct((t, d), F32)] + ([jax.ShapeDtypeStruct((t, d), BF16)] if with_bf16 else []) + [
        jax.ShapeDtypeStruct((1, d), F32)]
    out_specs = [row] + ([row] if with_bf16 else []) + [vec]
    return pl.pallas_call(
        body, name=name, grid=(t // tb,), in_specs=[row, row, vec, row], out_specs=out_specs, out_shape=out_shape,
        compiler_params=_cparams(("arbitrary",), tb * d * 36))(dy, x, g, resid)


def _final(h2, tgt, gate, pe, g):
    t, d = h2.shape
    tb = _row_block(t, d * 20)

    def body(h_ref, t_ref, gate_ref, pe_ref, g_ref, dh_ref, dpe_ref, dgl_ref, loss_ref, dfg_ref, dbg_ref):
        i = pl.program_id(0)
        hv = h_ref[...]
        r = lax.rsqrt(jnp.mean(hv * hv, axis=-1, keepdims=True) + EPS)
        n = hv * r
        gv = g_ref[...]
        err = n * gv - t_ref[...]
        loss_part = 0.5 * jnp.sum(jnp.mean(err * err, axis=-1, keepdims=True), axis=0, keepdims=True)
        dy = err * (1.0 / d)
        dn = dy * gv
        dh = r * (dn - n * jnp.mean(dn * n, axis=-1, keepdims=True))
        dh_ref[...] = dh
        gt = gate_ref[...].astype(F32)
        dpe_ref[...] = (dh * gt).astype(BF16)
        dgl = dh * pe_ref[...].astype(F32) * gt * (1.0 - gt)
        dgl_ref[...] = dgl.astype(BF16)
        dfg = jnp.sum(dy * n, axis=0, keepdims=True)
        dbg = jnp.sum(dgl, axis=0, keepdims=True)

        @pl.when(i == 0)
        def _():
            loss_ref[...] = jnp.broadcast_to(loss_part, loss_ref.shape)
            dfg_ref[...] = dfg
            dbg_ref[...] = dbg

        @pl.when(i > 0)
        def _():
            loss_ref[...] += jnp.broadcast_to(loss_part, loss_ref.shape)
            dfg_ref[...] += dfg
            dbg_ref[...] += dbg

    row = pl.BlockSpec((tb, d), lambda i: (i, 0))
    vec = pl.BlockSpec((1, d), lambda i: (0, 0))
    return pl.pallas_call(
        body, name="final_loss_bwd", grid=(t // tb,), in_specs=[row, row, row, row, vec],
        out_specs=[row, row, row, pl.BlockSpec((1, LANES), lambda i: (0, 0)), vec, vec],
        out_shape=[jax.ShapeDtypeStruct((t, d), F32), jax.ShapeDtypeStruct((t, d), BF16), jax.ShapeDtypeStruct((t, d), BF16),
                   jax.ShapeDtypeStruct((1, LANES), F32), jax.ShapeDtypeStruct((1, d), F32), jax.ShapeDtypeStruct((1, d), F32)],
        compiler_params=_cparams(("arbitrary",), tb * d * 40))(h2, tgt, gate, pe, g)


def _post_fwd(o, yc, proj7, gains):
    t, a = o.shape
    tb = _row_block(t, a * 14)
    nb = t // tb

    def body(o_ref, yc_ref, z_ref, g_ref, out_ref):
        s = pl.program_id(0)
        xin = jnp.where(s == 0, o_ref[...], yc_ref[...])
        r = lax.rsqrt(jnp.mean(xin * xin, axis=-1, keepdims=True) + EPS)
        z = z_ref[...].astype(F32)
        out_ref[...] = (xin * r * g_ref[...] * (z * _sigmoid(z))).astype(BF16)

    return pl.pallas_call(
        body, name="post_fwd", grid=(2, nb),
        in_specs=[pl.BlockSpec((tb, a), lambda s, i: (i * (1 - s) + (nb - 1) * s, 0)),
                  pl.BlockSpec((tb, a), lambda s, i: (i * s, 0)),
                  pl.BlockSpec((None, tb, a), lambda s, i: (3 + 3 * s, i, 0)),
                  pl.BlockSpec((None, 1, a), lambda s, i: (s, 0, 0))],
        out_specs=pl.BlockSpec((tb, a), lambda s, i: (i, s)),
        out_shape=jax.ShapeDtypeStruct((t, 2 * a), BF16),
        compiler_params=_cparams(("arbitrary", "arbitrary"), tb * a * 28))(o, yc, proj7, gains)


def _post_bwd(dycat, o, yc, proj7, gains):
    t, a = o.shape
    tb = _row_block(t, a * 24)
    nb = t // tb

    def body(dy_ref, o_ref, yc_ref, z_ref, g_ref, dz_ref, dx_ref, dg_ref, cs_ref):
        s = pl.program_id(0)
        i = pl.program_id(1)
        xin = jnp.where(s == 0, o_ref[...], yc_ref[...])
        r = lax.rsqrt(jnp.mean(xin * xin, axis=-1, keepdims=True) + EPS)
        n = xin * r
        z = z_ref[...].astype(F32)
        sg = _sigmoid(z)
        gv = g_ref[...]
        dy = dy_ref[...]
        dz_ref[...] = (dy * (n * gv) * (sg * (1.0 + z * (1.0 - sg)))).astype(BF16)
        dyn = dy * (z * sg)
        dn = dyn * gv
        dx = r * (dn - n * jnp.mean(dn * n, axis=-1, keepdims=True))
        dxb = dx.astype(BF16)
        dx_ref[...] = dxb
        dg = jnp.sum(dyn * n, axis=0, keepdims=True)
        cs = jnp.sum(dxb.astype(F32), axis=0, keepdims=True)

        @pl.when(i == 0)
        def _():
            dg_ref[...] = dg
            cs_ref[...] = cs

        @pl.when(i > 0)
        def _():
            dg_ref[...] += dg
            cs_ref[...] += cs

    vec = pl.BlockSpec((None, 1, a), lambda s, i: (s, 0, 0))
    return pl.pallas_call(
        body, name="post_bwd", grid=(2, nb),
        in_specs=[pl.BlockSpec((tb, a), lambda s, i: (i, s)),
                  pl.BlockSpec((tb, a), lambda s, i: (i * (1 - s) + (nb - 1) * s, 0)),
                  pl.BlockSpec((tb, a), lambda s, i: (i * s, 0)),
                  pl.BlockSpec((None, tb, a), lambda s, i: (3 + 3 * s, i, 0)),
                  vec],
        out_specs=[pl.BlockSpec((None, tb, a), lambda s, i: (3 + 3 * s, i, 0)),
                   pl.BlockSpec((None, tb, a), lambda s, i: (s, i, 0)), vec, vec],
        out_shape=[jax.ShapeDtypeStruct((7, t, a), BF16), jax.ShapeDtypeStruct((2, t, a), BF16),
                   jax.ShapeDtypeStruct((2, 1, a), F32), jax.ShapeDtypeStruct((2, 1, a), F32)],
        compiler_params=_cparams(("arbitrary", "arbitrary"), tb * a * 48))(dycat, o, yc, proj7, gains)


def _ln_silu_fwd(u1, g, b):
    t, c = u1.shape
    tb = _row_block(t, c * 6)

    def body(u_ref, g_ref, b_ref, o_ref):
        u = u_ref[...]
        mu = jnp.mean(u, axis=-1, keepdims=True)
        xc = u - mu
        rstd = lax.rsqrt(jnp.mean(xc * xc, axis=-1, keepdims=True) + EPS)
        u2 = xc * rstd * g_ref[...] + b_ref[...]
        o_ref[...] = (u2 * _sigmoid(u2)).astype(BF16)

    row = pl.BlockSpec((tb, c), lambda i: (i, 0))
    vec = pl.BlockSpec((1, c), lambda i: (0, 0))
    return pl.pallas_call(
        body, name="ln_silu_fwd", grid=(t // tb,), in_specs=[row, vec, vec], out_specs=row,
        out_shape=jax.ShapeDtypeStruct((t, c), BF16), compiler_params=_cparams(("parallel",), tb * c * 12))(u1, g, b)


def _ln_silu_bwd(du3, u1, g, b):
    t, c = u1.shape
    tb = _row_block(t, c * 12)

    def body(d_ref, u_ref, g_ref, b_ref, du_ref, dg_ref, db_ref, cs_ref):
        i = pl.program_id(0)
        u = u_ref[...]
        mu = jnp.mean(u, axis=-1, keepdims=True)
        xc = u - mu
        rstd = lax.rsqrt(jnp.mean(xc * xc, axis=-1, keepdims=True) + EPS)
        xh = xc * rstd
        gv = g_ref[...]
        u2 = xh * gv + b_ref[...]
        sg = _sigmoid(u2)
        du2 = d_ref[...] * (sg * (1.0 + u2 * (1.0 - sg)))
        dxh = du2 * gv
        du1 = rstd * (dxh - jnp.mean(dxh, axis=-1, keepdims=True) - xh * jnp.mean(dxh * xh, axis=-1, keepdims=True))
        du_ref[...] = du1
        dg = jnp.sum(du2 * xh, axis=0, keepdims=True)
        db = jnp.sum(du2, axis=0, keepdims=True)
        cs = jnp.sum(du1, axis=0, keepdims=True)

        @pl.when(i == 0)
        def _():
            dg_ref[...] = dg
            db_ref[...] = db
            cs_ref[...] = cs

        @pl.when(i > 0)
        def _():
            dg_ref[...] += dg
            db_ref[...] += db
            cs_ref[...] += cs

    row = pl.BlockSpec((tb, c), lambda i: (i, 0))
    vec = pl.BlockSpec((1, c), lambda i: (0, 0))
    vs = jax.ShapeDtypeStruct((1, c), F32)
    return pl.pallas_call(
        body, name="ln_silu_bwd", grid=(t // tb,), in_specs=[row, row, vec, vec], out_specs=[row, vec, vec, vec],
        out_shape=[jax.ShapeDtypeStruct((t, c), F32), vs, vs, vs],
        compiler_params=_cparams(("arbitrary",), tb * c * 24))(du3, u1, g, b)


def _conv_block(t):
    tb = 512
    while t % tb:
        tb //= 2
    assert tb >= HALO
    return tb


def _conv_fwd(proj7, w_dw, b_dw):
    _, t, c = proj7.shape
    tb = _conv_block(t)
    hb = tb // HALO
    cw = CONV_COLS

    def body(a_ref, g_ref, ah_ref, gh_ref, w_ref, b_ref, u1_ref, ext):
        i = pl.program_id(1)
        halo = ah_ref[...].astype(F32) * _sigmoid(gh_ref[...].astype(F32))
        ext[0:HALO, :] = jnp.where(i > 0, halo, 0.0)
        ext[HALO:HALO + tb, :] = a_ref[...].astype(F32) * _sigmoid(g_ref[...].astype(F32))
        bias = b_ref[...]
        for r0 in range(0, tb, CONV_ROWS):
            acc = jnp.broadcast_to(bias, (CONV_ROWS, cw))
            for k in range(CONV_K):
                off = r0 + HALO - (CONV_K - 1) + k
                acc = acc + ext[off:off + CONV_ROWS, :] * w_ref[k:k + 1, :]
            u1_ref[r0:r0 + CONV_ROWS, :] = acc

    blk = lambda seg: pl.BlockSpec((None, tb, cw), lambda j, i: (seg, i, j))
    halo = lambda seg: pl.BlockSpec((None, HALO, cw), lambda j, i: (seg, jnp.maximum(i * hb - 1, 0), j))
    return pl.pallas_call(
        body, name="conv_fwd", grid=(c // cw, t // tb),
        in_specs=[blk(4), blk(5), halo(4), halo(5), pl.BlockSpec((32, cw), lambda j, i: (0, j)),
                  pl.BlockSpec((1, cw), lambda j, i: (0, j))],
        out_specs=pl.BlockSpec((tb, cw), lambda j, i: (i, j)),
        out_shape=jax.ShapeDtypeStruct((t, c), F32),
        scratch_shapes=[pltpu.VMEM((tb + HALO, cw), F32)],
        compiler_params=_cparams(("parallel", "arbitrary"), tb * cw * 16))(proj7, proj7, proj7, proj7, w_dw, b_dw)


def _conv_bwd(dproj7, du1, proj7, w_dw):
    _, t, c = proj7.shape
    tb = _conv_block(t)
    hb = tb // HALO
    nb = t // tb
    cw = CONV_COLS

    def body(dp_ref, du_ref, duh_ref, a_ref, g_ref, ah_ref, gh_ref, w_ref, out_ref, dw_ref, dext, uext, dwp):
        del dp_ref
        i = pl.program_id(1)
        dext[0:tb, :] = du_ref[...]
        dext[tb:tb + HALO, :] = jnp.where(i < nb - 1, duh_ref[...], 0.0)
        halo = ah_ref[...].astype(F32) * _sigmoid(gh_ref[...].astype(F32))
        uext[0:HALO, :] = jnp.where(i > 0, halo, 0.0)
        for r0 in range(0, tb, CONV_ROWS):
            a_c = a_ref[r0:r0 + CONV_ROWS, :].astype(F32)
            s_c = _sigmoid(g_ref[r0:r0 + CONV_ROWS, :].astype(F32))
            uext[HALO + r0:HALO + r0 + CONV_ROWS, :] = a_c * s_c
            acc = jnp.zeros((CONV_ROWS, cw), F32)
            for k in range(CONV_K):
                off = r0 + (CONV_K - 1) - k
                acc = acc + dext[off:off + CONV_ROWS, :] * w_ref[k:k + 1, :]
            out_ref[0, r0:r0 + CONV_ROWS, :] = (acc * s_c).astype(BF16)
            out_ref[1, r0:r0 + CONV_ROWS, :] = (acc * a_c * s_c * (1.0 - s_c)).astype(BF16)
        for k in range(CONV_K):
            acc = jnp.zeros((8, cw), F32)
            for r0 in range(0, tb, CONV_ROWS):
                off = r0 + HALO - (CONV_K - 1) + k
                prod = dext[r0:r0 + CONV_ROWS, :] * uext[off:off + CONV_ROWS, :]
                acc = acc + jnp.sum(prod.reshape(CONV_ROWS // 8, 8, cw), axis=0)
            dwp[k:k + 1, :] = jnp.sum(acc, axis=0, keepdims=True)
        dwp[CONV_K:32, :] = jnp.zeros((32 - CONV_K, cw), F32)

        @pl.when(i == 0)
        def _():
            dw_ref[...] = dwp[...]

        @pl.when(i > 0)
        def _():
            dw_ref[...] += dwp[...]

    blk = lambda seg: pl.BlockSpec((None, tb, cw), lambda j, i: (seg, i, j))
    halo = lambda seg: pl.BlockSpec((None, HALO, cw), lambda j, i: (seg, jnp.maximum(i * hb - 1, 0), j))
    est = tb * cw * 40
    return pl.pallas_call(
        body, name="conv_bwd", grid=(c // cw, nb),
        in_specs=[pl.BlockSpec(memory_space=pl.ANY),
                  pl.BlockSpec((tb, cw), lambda j, i: (i, j)),
                  pl.BlockSpec((HALO, cw), lambda j, i: (jnp.minimum((i + 1) * hb, nb * hb - 1), j)),
                  blk(4), blk(5), halo(4), halo(5), pl.BlockSpec((32, cw), lambda j, i: (0, j))],
        out_specs=[pl.BlockSpec((2, tb, cw), lambda j, i: (2, i, j)), pl.BlockSpec((32, cw), lambda j, i: (0, j))],
        out_shape=[jax.ShapeDtypeStruct(dproj7.shape, BF16), jax.ShapeDtypeStruct((32, c), F32)],
        scratch_shapes=[pltpu.VMEM((tb + HALO, cw), F32), pltpu.VMEM((tb + HALO, cw), F32), pltpu.VMEM((32, cw), F32)],
        input_output_aliases={0: 0},
        compiler_params=_cparams(("parallel", "arbitrary"), est))(dproj7, du1, du1, proj7, proj7, proj7, proj7, w_dw)


class _Comm:
    def __init__(self, inputs, out_shape, sems, start, finish, aliased=False):
        self.inputs, self.out_shape, self.sems, self.start, self.finish, self.aliased = inputs, out_shape, sems, start, finish, aliased


def _pcall(body, *, name, grid, in_specs, out_specs, out_shape, operands, scratch=(), sem, est, comm=None, aliases=None):
    aliases = dict(aliases or {})
    if comm is None:
        return pl.pallas_call(body, name=name, grid=grid, in_specs=in_specs, out_specs=out_specs, out_shape=out_shape,
                              scratch_shapes=list(scratch), input_output_aliases=aliases,
                              compiler_params=_cparams(sem, est))(*operands)
    n_in, n_out, n_scr = len(in_specs), len(out_specs), len(scratch)
    n_ci, n_co, n_sem = len(comm.inputs), len(comm.out_shape), len(comm.sems)
    last_ids = [g - 1 for g in grid]

    def wrapped(*refs):
        ins, cin = refs[:n_in], refs[n_in:n_in + n_ci]
        outs = refs[n_in + n_ci:n_in + n_ci + n_out]
        cout = refs[n_in + n_ci + n_out:n_in + n_ci + n_out + n_co]
        scr = refs[n_in + n_ci + n_out + n_co:n_in + n_ci + n_out + n_co + n_scr]
        sems = refs[len(refs) - n_sem:]
        ids = [pl.program_id(k) for k in range(len(grid))]
        first = functools.reduce(jnp.logical_and, [i == 0 for i in ids])
        last = functools.reduce(jnp.logical_and, [i == l for i, l in zip(ids, last_ids)])

        @pl.when(first)
        def _():
            comm.start(cin, cout, sems)

        body(*ins, *outs, *scr)

        @pl.when(last)
        def _():
            comm.finish(cin, cout, sems)

    hbm = pl.BlockSpec(memory_space=pl.ANY)
    if comm.aliased:
        aliases.update({n_in + k: n_out + k for k in range(n_ci)})
    res = pl.pallas_call(
        wrapped, name=name, grid=grid, in_specs=[*in_specs, *([hbm] * n_ci)], out_specs=[*out_specs, *([hbm] * n_co)],
        out_shape=[*out_shape, *comm.out_shape], scratch_shapes=[*scratch, *comm.sems], input_output_aliases=aliases,
        compiler_params=_cparams(("arbitrary",) * len(grid), est))(*operands, *comm.inputs)
    return list(res[:n_out]), list(res[n_out:])


def _rel_index():
    j = np.arange(KB)
    key_minus_query = np.where(j < KB - CHUNK, j, j - KB) - QB
    return (np.clip(-key_minus_query, -MAX_REL, MAX_REL) + MAX_REL).astype(np.int32)


def _row_bits(shape):
    return lax.broadcasted_iota(jnp.int32, shape, 0)


def _bias_build(tvec):
    h = tvec.shape[0]

    def body(t_ref, o_ref):
        x = jnp.broadcast_to(t_ref[...], (QB, KB))
        row = _row_bits((QB, KB))
        for bit in range(QB.bit_length() - 1):
            x = jnp.where(((row >> bit) & 1) == 1, pltpu.roll(x, 1 << bit, 1), x)
        qc = row // CHUNK
        kc = lax.broadcasted_iota(jnp.int32, (QB, KB), 1) // CHUNK
        o_ref[...] = jnp.where((kc >= qc) & (kc <= qc + LEFT_CHUNKS), x, NEG_INF)

    return pl.pallas_call(
        body, name="bias_build", grid=(h,), in_specs=[pl.BlockSpec((None, 1, KB), lambda i: (i, 0, 0))],
        out_specs=pl.BlockSpec((None, QB, KB), lambda i: (i, 0, 0)), out_shape=jax.ShapeDtypeStruct((h, QB, KB), F32),
        compiler_params=_cparams(("parallel",), QB * KB * 16))(tvec)


def _bias_grad(ds):
    h = ds.shape[0]

    def body(d_ref, o_ref):
        x = d_ref[...]
        row = _row_bits((QB, KB))
        for bit in range(QB.bit_length() - 1):
            x = jnp.where(((row >> bit) & 1) == 1, pltpu.roll(x, KB - (1 << bit), 1), x)
        o_ref[...] = jnp.sum(x, axis=0, keepdims=True)

    return pl.pallas_call(
        body, name="bias_grad", grid=(h,), in_specs=[pl.BlockSpec((None, QB, KB), lambda i: (i, 0, 0))],
        out_specs=pl.BlockSpec((None, 1, KB), lambda i: (i, 0, 0)), out_shape=jax.ShapeDtypeStruct((h, 1, KB), F32),
        compiler_params=_cparams(("parallel",), QB * KB * 16))(ds)


def _attn_specs(a, nb, clamp):
    hpa = a // HEAD_DIM
    cur = (lambda i: jnp.minimum(i, nb - 1)) if clamp else (lambda i: i)
    prev = lambda i: jnp.maximum(cur(i) - 1, 0)
    q = pl.BlockSpec((None, QB, HEAD_DIM), lambda h, i: (0, cur(i), h))
    kp = pl.BlockSpec((None, QB, HEAD_DIM), lambda h, i: (1, prev(i), h))
    kc = pl.BlockSpec((None, QB, HEAD_DIM), lambda h, i: (1, cur(i), h))
    vp = pl.BlockSpec((None, QB, HEAD_DIM), lambda h, i: (2, prev(i), h))
    vc = pl.BlockSpec((None, QB, HEAD_DIM), lambda h, i: (2, cur(i), h))
    bias = pl.BlockSpec((None, QB, KB), lambda h, i: (h, 0, 0))
    del hpa
    return q, kp, kc, vp, vc, bias


def _scores(q, kp, kc, b_ref, first):
    scale = HEAD_DIM ** -0.5
    pen = jnp.where(first, NEG_INF, 0.0).astype(F32)
    sp = lax.dot_general(q, kp, NT, preferred_element_type=F32) * scale + b_ref[:, 0:QB] + pen
    sc = lax.dot_general(q, kc, NT, preferred_element_type=F32) * scale + b_ref[:, QB:KB]
    m = jnp.maximum(jnp.max(sp, axis=-1, keepdims=True), jnp.max(sc, axis=-1, keepdims=True))
    ep = jnp.exp(sp - m)
    ec = jnp.exp(sc - m)
    l = jnp.sum(ep, axis=-1, keepdims=True) + jnp.sum(ec, axis=-1, keepdims=True)
    return ep, ec, l


def _attn_fwd(proj7, bias, comm=None):
    _, t, a = proj7.shape
    nb = t // QB
    h = a // HEAD_DIM

    def body(q_ref, kp_ref, kc_ref, vp_ref, vc_ref, b_ref, o_ref):
        i = pl.program_id(1)
        ep, ec, l = _scores(q_ref[...], kp_ref[...], kc_ref[...], b_ref, i == 0)
        o = (jnp.dot(ep.astype(BF16), vp_ref[...], preferred_element_type=F32)
             + jnp.dot(ec.astype(BF16), vc_ref[...], preferred_element_type=F32))
        o_ref[...] = o / l

    res = _pcall(
        body, name="attn_fwd", grid=(h, nb), in_specs=list(_attn_specs(a, nb, False)),
        out_specs=[pl.BlockSpec((QB, HEAD_DIM), lambda hh, i: (i, hh))],
        out_shape=[jax.ShapeDtypeStruct((t, a), F32)], operands=(proj7, proj7, proj7, proj7, proj7, bias),
        sem=("parallel", "arbitrary"), est=QB * KB * 40, comm=comm)
    return res[0] if comm is None else (res[0][0], res[1])


def _attn_bwd(dproj7, proj7, do, bias):
    _, t, a = proj7.shape
    nb = t // QB
    h = a // HEAD_DIM
    scale = HEAD_DIM ** -0.5

    def body(dp_ref, q_ref, kp_ref, kc_ref, vp_ref, vc_ref, b_ref, do_ref, out_ref, ds_ref, dq_c, dk_c, dv_c):
        del dp_ref
        i = pl.program_id(1)

        @pl.when(i < nb)
        def _():
            q = q_ref[...]
            kp = kp_ref[...]
            kc = kc_ref[...]
            do = do_ref[...]
            ep, ec, l = _scores(q, kp, kc, b_ref, i == 0)
            inv = 1.0 / l
            pp = ep * inv
            pc = ec * inv
            dpp = lax.dot_general(do, vp_ref[...], NT, preferred_element_type=F32)
            dpc = lax.dot_general(do, vc_ref[...], NT, preferred_element_type=F32)
            delta = jnp.sum(pp * dpp, axis=-1, keepdims=True) + jnp.sum(pc * dpc, axis=-1, keepdims=True)
            dsp = pp * (dpp - delta)
            dsc = pc * (dpc - delta)

            @pl.when(i == 0)
            def _():
                ds_ref[:, 0:QB] = dsp
                ds_ref[:, QB:KB] = dsc

            @pl.when(i > 0)
            def _():
                ds_ref[:, 0:QB] += dsp
                ds_ref[:, QB:KB] += dsc

            dspb = (dsp * scale).astype(BF16)
            dscb = (dsc * scale).astype(BF16)
            dq = jnp.dot(dspb, kp, preferred_element_type=F32) + jnp.dot(dscb, kc, preferred_element_type=F32)
            dkp = lax.dot_general(dspb, q, TN, preferred_element_type=F32)
            dkc = lax.dot_general(dscb, q, TN, preferred_element_type=F32)
            dvp = lax.dot_general(pp.astype(BF16), do, TN, preferred_element_type=F32)
            dvc = lax.dot_general(pc.astype(BF16), do, TN, preferred_element_type=F32)

            @pl.when(i > 0)
            def _():
                out_ref[0] = dq_c[...].astype(BF16)
                out_ref[1] = (dk_c[...] + dkp).astype(BF16)
                out_ref[2] = (dv_c[...] + dvp).astype(BF16)

            dq_c[...] = dq
            dk_c[...] = dkc
            dv_c[...] = dvc

        @pl.when(i == nb)
        def _():
            out_ref[0] = dq_c[...].astype(BF16)
            out_ref[1] = dk_c[...].astype(BF16)
            out_ref[2] = dv_c[...].astype(BF16)

    q, kp, kc, vp, vc, bspec = _attn_specs(a, nb, True)
    return pl.pallas_call(
        body, name="attn_bwd", grid=(h, nb + 1),
        in_specs=[pl.BlockSpec(memory_space=pl.ANY), q, kp, kc, vp, vc, bspec,
                  pl.BlockSpec((None, QB, HEAD_DIM), lambda hh, i: (0, jnp.minimum(i, nb - 1), hh))],
        out_specs=[pl.BlockSpec((3, QB, HEAD_DIM), lambda hh, i: (0, jnp.maximum(i - 1, 0), hh)),
                   pl.BlockSpec((None, QB, KB), lambda hh, i: (hh, 0, 0))],
        out_shape=[jax.ShapeDtypeStruct(dproj7.shape, BF16), jax.ShapeDtypeStruct((h, QB, KB), F32)],
        scratch_shapes=[pltpu.VMEM((QB, HEAD_DIM), F32)] * 3,
        input_output_aliases={0: 0},
        compiler_params=_cparams(("parallel", "arbitrary"), QB * KB * 60))(dproj7, proj7, proj7, proj7, proj7, proj7, bias, do)


def _tile(n, pref):
    t = min(n, pref)
    while n % t:
        t //= 2
    return t


def _mm(name, a, b, dn, grid, a_spec, b_spec, acc_shape, out_shape, out_specs, epilogue, extra=(), extra_specs=(), est=0,
        comm=None):
    nk = grid[2]
    n_extra = len(extra)
    single = not isinstance(out_shape, (list, tuple))
    if single:
        out_shape, out_specs = [out_shape], [out_specs]

    def body(*refs):
        a_ref, b_ref = refs[0], refs[1]
        ex = refs[2:2 + n_extra]
        outs = refs[2 + n_extra:-1]
        acc = refs[-1]
        k = pl.program_id(2)
        prod = lax.dot_general(a_ref[...].astype(BF16), b_ref[...].astype(BF16), dn, preferred_element_type=F32)

        @pl.when(k == 0)
        def _():
            acc[...] = prod

        @pl.when(k > 0)
        def _():
            acc[...] += prod

        @pl.when(k == nk - 1)
        def _():
            epilogue(acc[...], ex, outs)

    res = _pcall(body, name=name, grid=grid, in_specs=[a_spec, b_spec, *extra_specs], out_specs=list(out_specs),
                 out_shape=list(out_shape), operands=(a, b, *extra), scratch=[pltpu.VMEM(acc_shape, F32)],
                 sem=("parallel", "parallel", "arbitrary"), est=est, comm=comm)
    if comm is None:
        return res[0] if single else res
    return (res[0][0] if single else res[0]), res[1]


def _store(dtype):
    def ep(acc, ex, outs):
        outs[0][...] = acc.astype(dtype)
    return ep


def _mm_est(tm, tn, tk, out_bytes):
    return 2 * (tm * tk * 4 + tk * tn * 2) + tm * tn * 4 + 2 * tm * tn * out_bytes


def _mm_nn(name, a, b, out_dtype, bias=None, resid=None):
    m, k = a.shape
    n = b.shape[1]
    tm, tn, tk = _tile(m, 1024), _tile(n, 1024), _tile(k, 1024)
    extra, especs = [], []
    if bias is not None:
        extra.append(bias)
        especs.append(pl.BlockSpec((1, tn), lambda i, j, kk: (0, j)))
    if resid is not None:
        extra.append(resid)
        especs.append(pl.BlockSpec((tm, tn), lambda i, j, kk: (i, j)))

    def ep(acc, ex, outs):
        for r in ex:
            acc = acc + r[...]
        outs[0][...] = acc.astype(out_dtype)

    return _mm(name, a, b, NN, (m // tm, n // tn, k // tk),
               pl.BlockSpec((tm, tk), lambda i, j, kk: (i, kk)), pl.BlockSpec((tk, tn), lambda i, j, kk: (kk, j)),
               (tm, tn), jax.ShapeDtypeStruct((m, n), out_dtype), pl.BlockSpec((tm, tn), lambda i, j, kk: (i, j)),
               ep, extra, especs, _mm_est(tm, tn, tk, 12))


def _mm_nt(name, a, b, out_dtype):
    m, k = a.shape
    n = b.shape[0]
    tm, tn, tk = _tile(m, 1024), _tile(n, 1024), _tile(k, 1024)
    return _mm(name, a, b, NT, (m // tm, n // tn, k // tk),
               pl.BlockSpec((tm, tk), lambda i, j, kk: (i, kk)), pl.BlockSpec((tn, tk), lambda i, j, kk: (j, kk)),
               (tm, tn), jax.ShapeDtypeStruct((m, n), out_dtype), pl.BlockSpec((tm, tn), lambda i, j, kk: (i, j)),
               _store(out_dtype), est=_mm_est(tm, tn, tk, 4))


def _mm_tn(name, a, b, out_dtype):
    k, m = a.shape
    n = b.shape[1]
    tm, tn, tk = _tile(m, 1024), _tile(n, 1024), _tile(k, 1024)
    return _mm(name, a, b, TN, (m // tm, n // tn, k // tk),
               pl.BlockSpec((tk, tm), lambda i, j, kk: (kk, i)), pl.BlockSpec((tk, tn), lambda i, j, kk: (kk, j)),
               (tm, tn), jax.ShapeDtypeStruct((m, n), out_dtype), pl.BlockSpec((tm, tn), lambda i, j, kk: (i, j)),
               _store(out_dtype), est=_mm_est(tm, tn, tk, 4))


def _proj_fwd(xn, w_in, seg, comm=None):
    t, d = xn.shape
    ncol = w_in.shape[1]
    tm, tn, tk = _tile(t, 1024), _tile(seg, 1024), _tile(d, 1024)
    per = seg // tn
    return _mm("proj_fwd", xn, w_in, NN, (t // tm, ncol // tn, d // tk),
               pl.BlockSpec((tm, tk), lambda i, j, kk: (i, kk)), pl.BlockSpec((tk, tn), lambda i, j, kk: (kk, j)),
               (tm, tn), jax.ShapeDtypeStruct((ncol // seg, t, seg), BF16),
               pl.BlockSpec((None, tm, tn), lambda i, j, kk: (j // per, i, j % per)),
               _store(BF16), est=_mm_est(tm, tn, tk, 2), comm=comm)


def _proj_bwd_x(dproj7, w_in, comm=None):
    nseg, t, seg = dproj7.shape
    d = w_in.shape[0]
    tm, tn, tk = _tile(t, 1024), _tile(d, 1024), _tile(seg, 1024)
    per = seg // tk
    return _mm("proj_bwd_x", dproj7, w_in, NT, (t // tm, d // tn, nseg * per),
               pl.BlockSpec((None, tm, tk), lambda i, j, kk: (kk // per, i, kk % per)),
               pl.BlockSpec((tn, tk), lambda i, j, kk: (j, kk)),
               (tm, tn), jax.ShapeDtypeStruct((t, d), F32), pl.BlockSpec((tm, tn), lambda i, j, kk: (i, j)),
               _store(F32), est=_mm_est(tm, tn, tk, 4), comm=comm)


def _proj_bwd_w(xn, dproj7, comm=None):
    nseg, t, seg = dproj7.shape
    d = xn.shape[1]
    tm, tn, tk = _tile(d, 1024), _tile(seg, 1024), _tile(t, 1024)
    per = seg // tn
    return _mm("proj_bwd_w", xn, dproj7, TN, (d // tm, nseg * per, t // tk),
               pl.BlockSpec((tk, tm), lambda i, j, kk: (kk, i)),
               pl.BlockSpec((None, tk, tn), lambda i, j, kk: (j // per, kk, j % per)),
               (tm, tn), jax.ShapeDtypeStruct((d, nseg * seg), BF16), pl.BlockSpec((tm, tn), lambda i, j, kk: (i, j)),
               _store(BF16), est=_mm_est(tm, tn, tk, 2), comm=comm)


def _gate_fwd(hn, w_g, b_g, h1, p, w_ple):
    t, d = hn.shape
    pd = p.shape[1]
    tm, tn, tk = _tile(t, 1024), _tile(d, 1024), _tile(d, 1024)

    def ep(acc, ex, outs):
        b_ref, h1_ref, p_ref, wp_ref = ex
        gate = _sigmoid(acc + b_ref[...])
        pe = jnp.dot(p_ref[...].astype(BF16), wp_ref[...], preferred_element_type=F32)
        outs[0][...] = h1_ref[...] + gate * pe
        outs[1][...] = gate.astype(BF16)
        outs[2][...] = pe.astype(BF16)

    tile = pl.BlockSpec((tm, tn), lambda i, j, kk: (i, j))
    return _mm("gate_fwd", hn, w_g, NN, (t // tm, d // tn, d // tk),
               pl.BlockSpec((tm, tk), lambda i, j, kk: (i, kk)), pl.BlockSpec((tk, tn), lambda i, j, kk: (kk, j)),
               (tm, tn),
               [jax.ShapeDtypeStruct((t, d), F32), jax.ShapeDtypeStruct((t, d), BF16), jax.ShapeDtypeStruct((t, d), BF16)],
               [tile, tile, tile], ep, [b_g, h1, p, w_ple],
               [pl.BlockSpec((1, tn), lambda i, j, kk: (0, j)), tile, pl.BlockSpec((tm, pd), lambda i, j, kk: (i, 0)),
                pl.BlockSpec((pd, tn), lambda i, j, kk: (0, j))],
               _mm_est(tm, tn, tk, 16) + 2 * (tm * pd * 4 + pd * tn * 2))


def _coords():
    return lax.axis_index("x"), lax.axis_index("y"), lax.axis_index("c")


def _row_slice(rows):
    return lambda ref, o: ref.at[pl.ds(pl.multiple_of(o * rows, rows), rows), :]


def _col_slice(cols):
    return lambda ref, o: ref.at[:, pl.ds(pl.multiple_of(o * cols, cols), cols)]


def _all_gather(shards, slicers, full_shapes):
    n = len(shards)

    def body(*refs):
        srcs, outs = refs[:n], refs[n:2 * n]
        send_sems, recv_sems, local_sems = refs[2 * n:]
        x, y, c = _coords()
        me, sibling = (x, y, c), (x, y, 1 - c)
        chips = [(1 - x, y), (x, 1 - y), (1 - x, 1 - y)]

        def copy(w, k, block, to, src=None):
            dst = slicers[w](outs[w], 4 * block[0] + 2 * block[1] + block[2])
            return pltpu.make_async_remote_copy(
                src_ref=dst if src is None else src, dst_ref=dst, send_sem=send_sems.at[w, k], recv_sem=recv_sems.at[w, k],
                device_id=to, device_id_type=MESH)

        mine = [pltpu.make_async_copy(srcs[w], slicers[w](outs[w], 4 * x + 2 * y + c), local_sems.at[w]) for w in range(n)]
        for cp in mine:
            cp.start()
        first = []
        for w in range(n):
            first.append(copy(w, 0, me, sibling, src=srcs[w]))
            first += [copy(w, 1 + j, me, (*chip, c), src=srcs[w]) for j, chip in enumerate(chips)]
        for cp in first:
            cp.start()
        passed = []
        for j, chip in enumerate(chips):
            for w in range(n):
                copy(w, 1 + j, (*chip, c), me).wait_recv()
                fwd = copy(w, 4 + j, (*chip, c), sibling)
                fwd.start()
                passed.append(fwd)
        for w in range(n):
            copy(w, 0, sibling, me).wait_recv()
            for j, chip in enumerate(chips):
                copy(w, 4 + j, (*chip, 1 - c), me).wait_recv()
        for cp in first + passed:
            cp.wait_send()
        for cp in mine:
            cp.wait()

    hbm = pl.BlockSpec(memory_space=pl.ANY)
    return pl.pallas_call(
        body, name="all_gather_weights",
        out_shape=[jax.ShapeDtypeStruct(s, a.dtype) for s, a in zip(full_shapes, shards)],
        in_specs=[hbm] * n, out_specs=[hbm] * n,
        scratch_shapes=[pltpu.SemaphoreType.DMA((n, 7)), pltpu.SemaphoreType.DMA((n, 7)), pltpu.SemaphoreType.DMA((n,))],
        compiler_params=pltpu.CompilerParams(has_side_effects=True))(*shards)


def _gather_ici_comm(shards, slicers, full_shapes):
    n = len(shards)

    def descs(cin, cout, sems):
        send_sems, recv_sems, local_sems = sems
        x, y, c = _coords()
        me, sibling = (x, y, c), (x, y, 1 - c)
        chips = [(1 - x, y), (x, 1 - y), (1 - x, 1 - y)]

        def copy(w, k, block, to, src=None):
            dst = slicers[w](cout[w], 4 * block[0] + 2 * block[1] + block[2])
            return pltpu.make_async_remote_copy(
                src_ref=dst if src is None else src, dst_ref=dst, send_sem=send_sems.at[w, k], recv_sem=recv_sems.at[w, k],
                device_id=to, device_id_type=MESH)

        mine = [pltpu.make_async_copy(cin[w], slicers[w](cout[w], 4 * x + 2 * y + c), local_sems.at[w]) for w in range(n)]
        sends, recvs = [], []
        for w in range(n):
            sends.append(copy(w, 0, me, sibling, src=cin[w]))
            recvs.append(copy(w, 0, sibling, me))
            for j, chip in enumerate(chips):
                sends.append(copy(w, 1 + j, me, (*chip, c), src=cin[w]))
                recvs.append(copy(w, 1 + j, (*chip, c), me))
        return mine, sends, recvs

    def start(cin, cout, sems):
        mine, sends, _ = descs(cin, cout, sems)
        for cp in mine + sends:
            cp.start()

    def finish(cin, cout, sems):
        mine, sends, recvs = descs(cin, cout, sems)
        for cp in recvs:
            cp.wait_recv()
        for cp in sends:
            cp.wait_send()
        for cp in mine:
            cp.wait()

    return _Comm(list(shards), [jax.ShapeDtypeStruct(s, a.dtype) for s, a in zip(full_shapes, shards)],
                 [pltpu.SemaphoreType.DMA((n, 4)), pltpu.SemaphoreType.DMA((n, 4)), pltpu.SemaphoreType.DMA((n,))], start, finish)


def _gather_d2d_comm(fulls, slicers):
    n = len(fulls)

    def descs(cout, sems):
        send_sems, recv_sems = sems
        x, y, c = _coords()
        sibling = (x, y, 1 - c)
        chips = [(1 - x, y), (x, 1 - y), (1 - x, 1 - y)]
        sends, recvs = [], []
        for w in range(n):
            for j, (px, py) in enumerate(chips):
                for core, out in ((c, sends), (1 - c, recvs)):
                    blk = slicers[w](cout[w], 4 * px + 2 * py + core)
                    out.append(pltpu.make_async_remote_copy(
                        src_ref=blk, dst_ref=blk, send_sem=send_sems.at[w, j], recv_sem=recv_sems.at[w, j],
                        device_id=sibling, device_id_type=MESH))
        return sends, recvs

    def start(cin, cout, sems):
        for cp in descs(cout, sems)[0]:
            cp.start()

    def finish(cin, cout, sems):
        sends, recvs = descs(cout, sems)
        for cp in recvs:
            cp.wait_recv()
        for cp in sends:
            cp.wait_send()

    return _Comm(list(fulls), [jax.ShapeDtypeStruct(f.shape, f.dtype) for f in fulls],
                 [pltpu.SemaphoreType.DMA((n, 3)), pltpu.SemaphoreType.DMA((n, 3))], start, finish, aliased=True)


def _chip_exchange_comm(pairsums):
    n = len(pairsums)

    def descs(cin, cout, sems):
        send_sems, recv_sems = sems
        x, y, c = _coords()
        copies = []
        for w in range(n):
            for r in (1, 2, 3):
                px, py = x ^ (r >> 1), y ^ (r & 1)
                copies.append(pltpu.make_async_remote_copy(
                    src_ref=cin[w].at[2 * px + py], dst_ref=cout[w].at[r - 1],
                    send_sem=send_sems.at[w, r - 1], recv_sem=recv_sems.at[w, r - 1], device_id=(px, py, c), device_id_type=MESH))
        return copies

    def start(cin, cout, sems):
        for cp in descs(cin, cout, sems):
            cp.start()

    def finish(cin, cout, sems):
        copies = descs(cin, cout, sems)
        for cp in copies:
            cp.wait_recv()
        for cp in copies:
            cp.wait_send()

    return _Comm(list(pairsums), [jax.ShapeDtypeStruct((3, *p.shape[1:]), p.dtype) for p in pairsums],
                 [pltpu.SemaphoreType.DMA((n, 3)), pltpu.SemaphoreType.DMA((n, 3))], start, finish)


def _pair_exchange(name, grads, slicers, shard_shapes):
    n = len(grads)

    def body(*refs):
        srcs, outs = refs[:n], refs[n:2 * n]
        send_sems, recv_sems = refs[2 * n:]
        x, y, c = _coords()
        sibling = (x, y, 1 - c)
        copies = []
        for w in range(n):
            for j in range(4):
                copies.append(pltpu.make_async_remote_copy(
                    src_ref=slicers[w](srcs[w], 2 * j + (1 - c)), dst_ref=outs[w].at[j],
                    send_sem=send_sems.at[w, j], recv_sem=recv_sems.at[w, j], device_id=sibling, device_id_type=MESH))
        for cp in copies:
            cp.start()
        for cp in copies:
            cp.wait_recv()
        for cp in copies:
            cp.wait_send()

    hbm = pl.BlockSpec(memory_space=pl.ANY)
    return pl.pallas_call(
        body, name=name,
        out_shape=[jax.ShapeDtypeStruct((4, *s), g.dtype) for s, g in zip(shard_shapes, grads)],
        in_specs=[hbm] * n, out_specs=[hbm] * n,
        scratch_shapes=[pltpu.SemaphoreType.DMA((n, 4)), pltpu.SemaphoreType.DMA((n, 4))],
        compiler_params=pltpu.CompilerParams(has_side_effects=True))(*grads)


def _pair_add(name, grad, recv, col_sharded, cidx):
    _, r, cc = recv.shape
    tr = _row_block(r, cc * 6, budget=8 << 20)
    nr = r // tr
    if col_sharded:
        g_spec = pl.BlockSpec((tr, cc), lambda j, i, s: (i, 2 * j + s[0]))
    else:
        g_spec = pl.BlockSpec((tr, cc), lambda j, i, s: ((2 * j + s[0]) * nr + i, 0))

    def body(s_ref, g_ref, r_ref, o_ref):
        del s_ref
        o_ref[...] = (g_ref[...].astype(F32) + r_ref[...].astype(F32)).astype(BF16)

    return pl.pallas_call(
        body, name=name,
        grid_spec=pltpu.PrefetchScalarGridSpec(
            num_scalar_prefetch=1, grid=(4, nr),
            in_specs=[g_spec, pl.BlockSpec((None, tr, cc), lambda j, i, s: (j, i, 0))],
            out_specs=pl.BlockSpec((None, tr, cc), lambda j, i, s: (j, i, 0))),
        out_shape=jax.ShapeDtypeStruct(recv.shape, BF16),
        compiler_params=_cparams(("parallel", "parallel"), tr * cc * 12))(cidx, grad, recv)


def _all_reduce_small(vec):
    r = vec.shape[0]

    def body(v_ref, o_ref, gath, send_sems, recv_sems):
        x, y, c = _coords()
        me = 4 * x + 2 * y + c
        copies = []
        for rel in range(1, N_DEV):
            peer = (x ^ (rel >> 2), y ^ ((rel >> 1) & 1), c ^ (rel & 1))
            copies.append(pltpu.make_async_remote_copy(
                src_ref=v_ref, dst_ref=gath.at[me], send_sem=send_sems.at[rel - 1], recv_sem=recv_sems.at[rel - 1],
                device_id=peer, device_id_type=MESH))
        for cp in copies:
            cp.start()
        gath[me] = v_ref[...]
        for rel in range(1, N_DEV):
            src = 4 * (x ^ (rel >> 2)) + 2 * (y ^ ((rel >> 1) & 1)) + (c ^ (rel & 1))
            pltpu.make_async_remote_copy(
                src_ref=v_ref, dst_ref=gath.at[src], send_sem=send_sems.at[rel - 1], recv_sem=recv_sems.at[rel - 1],
                device_id=(x, y, c), device_id_type=MESH).wait_recv()
        for cp in copies:
            cp.wait_send()
        acc = gath[0]
        for d in range(1, N_DEV):
            acc = acc + gath[d]
        o_ref[...] = acc

    vm = pl.BlockSpec(memory_space=pltpu.VMEM)
    return pl.pallas_call(
        body, name="all_reduce_small", out_shape=jax.ShapeDtypeStruct(vec.shape, F32), in_specs=[vm], out_specs=vm,
        scratch_shapes=[pltpu.VMEM((N_DEV, r, LANES), F32), pltpu.SemaphoreType.DMA((N_DEV - 1,)),
                        pltpu.SemaphoreType.DMA((N_DEV - 1,))],
        compiler_params=pltpu.CompilerParams(has_side_effects=True, vmem_limit_bytes=int(min(VMEM_CAP, 32 * r * LANES * 4 + (16 << 20)))))(vec)


def _adamw_math(w, g, m, v):
    m = ADAM_B1 * m + (1.0 - ADAM_B1) * g
    v = ADAM_B2 * v + (1.0 - ADAM_B2) * (g * g)
    m_hat = m / (1.0 - ADAM_B1 ** ADAM_STEP)
    v_hat = v / (1.0 - ADAM_B2 ** ADAM_STEP)
    delta = -ADAM_LR * (m_hat / (jnp.sqrt(v_hat) + ADAM_EPS) + ADAM_WD * w)
    return delta, m, v


def _adamw_big(name, pairsum, recv, w, m, v, chip_idx):
    r, cc = w.shape
    tr = _row_block(r, cc * 36, budget=16 << 20)

    def body(s_ref, p_ref, r_ref, w_ref, m_ref, v_ref, g_out, d_out, m_out, v_out):
        del s_ref
        g = p_ref[...].astype(F32)
        for k in range(3):
            g = g + r_ref[k].astype(F32)
        delta, m2, v2 = _adamw_math(w_ref[...], g, m_ref[...], v_ref[...])
        g_out[...] = g
        d_out[...] = delta
        m_out[...] = m2
        v_out[...] = v2

    tile = pl.BlockSpec((tr, cc), lambda i, s: (i, 0))
    sds = jax.ShapeDtypeStruct((r, cc), F32)
    return pl.pallas_call(
        body, name=name,
        grid_spec=pltpu.PrefetchScalarGridSpec(
            num_scalar_prefetch=1, grid=(r // tr,),
            in_specs=[pl.BlockSpec((None, tr, cc), lambda i, s: (s[0], i, 0)), pl.BlockSpec((3, tr, cc), lambda i, s: (0, i, 0)),
                      tile, tile, tile],
            out_specs=[tile, tile, tile, tile]),
        out_shape=[sds, sds, sds, sds],
        compiler_params=_cparams(("parallel",), tr * cc * 72))(chip_idx, pairsum, recv, w, m, v)


def _adamw_small(g, w, m, v):
    r = g.shape[0]

    def body(g_ref, w_ref, m_ref, v_ref, d_out, m_out, v_out):
        delta, m2, v2 = _adamw_math(w_ref[...], g_ref[...], m_ref[...], v_ref[...])
        d_out[...] = delta
        m_out[...] = m2
        v_out[...] = v2

    vm = pl.BlockSpec(memory_space=pltpu.VMEM)
    sds = jax.ShapeDtypeStruct((r, LANES), F32)
    return pl.pallas_call(body, name="adamw_small", out_shape=[sds, sds, sds], in_specs=[vm] * 4, out_specs=[vm] * 3,
                          compiler_params=_cparams(None, r * LANES * 28))(g, w, m, v)


def _pack(parts, rows):
    flat = jnp.concatenate([q.reshape(-1).astype(F32) for q in parts])
    return jnp.pad(flat, (0, rows * LANES - flat.shape[0])).reshape(rows, LANES)


def _unpack(packed, shapes):
    flat = packed.reshape(-1)
    out, off = [], 0
    for s in shapes:
        n = int(np.prod(s))
        out.append(flat[off:off + n].reshape(s))
        off += n
    return out


def kernel(x, p, norm_in_g, w_in, rel_table, w_dw, b_dw, conv_ln_g, conv_ln_b, w_pw, b_pw, attn_out_g, conv_out_g, w_out, ple_norm_g, w_ple_gate, b_ple_gate, w_ple, final_g, loss_target, m_norm_in_g, m_w_in, m_rel_table, m_w_dw, m_b_dw, m_conv_ln_g, m_conv_ln_b, m_w_pw, m_b_pw, m_attn_out_g, m_conv_out_g, m_w_out, m_ple_norm_g, m_w_ple_gate, m_b_ple_gate, m_w_ple, m_final_g, v_norm_in_g, v_w_in, v_rel_table, v_w_dw, v_b_dw, v_conv_ln_g, v_conv_ln_b, v_w_pw, v_b_pw, v_attn_out_g, v_conv_out_g, v_w_out, v_ple_norm_g, v_w_ple_gate, v_b_ple_gate, v_w_ple, v_final_g):
    t, d = x.shape[1], x.shape[2]
    a = d // 2
    ncol = 7 * a
    ns = ncol // N_DEV
    pd = p.shape[-1]
    heads = a // HEAD_DIM
    assert x.shape[0] == 1 and t % QB == 0 and ns % LANES == 0 and w_in.shape == (1, d, ns)
    cx, cy, cc_ = _coords()
    me = 4 * cx + 2 * cy + cc_
    core_idx = jnp.reshape(cc_, (1,)).astype(jnp.int32)
    chip_idx = jnp.reshape(2 * cx + cy, (1,)).astype(jnp.int32)

    x2, tgt, p2 = x[0], loss_target[0], p[0, 0]
    final_g2 = final_g.reshape(1, d)

    w_dw_pad = jnp.pad(w_dw[0], ((0, 32 - CONV_K), (0, 0)))
    (wf_in,) = _all_gather([_cast_bf16(w_in[0], "cast_w_in")], [_col_slice(ns)], [(d, ncol)])
    shards = [_cast_bf16(w_out[0], "cast_w_out"), _cast_bf16(w_ple_gate[0], "cast_w_gate"),
              _cast_bf16(w_pw[0], "cast_w_pw"), _cast_bf16(w_ple[0], "cast_w_ple"), w_dw_pad]
    slicers = [_row_slice(d // N_DEV), _row_slice(d // N_DEV), _row_slice(a // N_DEV), _col_slice(d // N_DEV), _col_slice(a // N_DEV)]
    full_shapes = [(d, d), (d, d), (a, a), (pd, d), (32, a)]

    xn = _rms_fwd(x2, norm_in_g, "rms_in_fwd")
    proj7, fulls = _proj_fwd(xn, wf_in, a, comm=_gather_ici_comm(shards, slicers, full_shapes))
    tvec = rel_table[0][:, _rel_index()].reshape(heads, 1, KB)
    bias = _bias_build(tvec)
    o, (wf_out, wf_gate, wf_pw, wf_ple, wf_dw) = _attn_fwd(proj7, bias, comm=_gather_d2d_comm(fulls, slicers))
    u1 = _conv_fwd(proj7, wf_dw, b_dw)
    u3 = _ln_silu_fwd(u1, conv_ln_g, conv_ln_b)
    yc = _mm_nn("pw_fwd", u3, wf_pw, F32, bias=b_pw)
    gains = jnp.stack([attn_out_g, conv_out_g])
    ycat = _post_fwd(o, yc, proj7, gains)
    h1 = _mm_nn("out_fwd", ycat, wf_out, F32, resid=x2)
    hn = _rms_fwd(h1, ple_norm_g, "rms_ple_fwd")
    h2, gate, pe = _gate_fwd(hn, wf_gate, b_ple_gate, h1, p2, wf_ple)

    dh2, dpe, dgl, loss_v, d_final_g, d_b_gate = _final(h2, tgt, gate, pe, final_g2)
    dhn = _mm_nt("gate_bwd_x", dgl, wf_gate, F32)
    dw_gate = _mm_tn("gate_bwd_w", hn, dgl, BF16)
    dw_ple = _mm_tn("ple_bwd_w", p2, dpe, BF16)
    dh1, dh1b, d_ple_norm_g = _rms_bwd(dhn, h1, ple_norm_g, dh2, "rms_ple_bwd", True)
    dycat = _mm_nt("out_bwd_x", dh1b, wf_out, F32)
    dw_out = _mm_tn("out_bwd_w", ycat, dh1b, BF16)
    dproj7, dxin, d_gains, cs_dxin = _post_bwd(dycat, o, yc, proj7, gains)
    dproj7, ds = _attn_bwd(dproj7, proj7, dxin, bias)
    dyc = dxin[1]
    du3 = _mm_nt("pw_bwd_x", dyc, wf_pw, F32)
    dw_pw = _mm_tn("pw_bwd_w", u3, dyc, BF16)
    du1, d_ln_g, d_ln_b, d_b_dw = _ln_silu_bwd(du3, u1, conv_ln_g, conv_ln_b)
    dproj7, d_w_dw = _conv_bwd(dproj7, du1, proj7, wf_dw)

    rest = [("w_out", dw_out, False, d // N_DEV, w_out, m_w_out, v_w_out),
            ("w_gate", dw_gate, False, d // N_DEV, w_ple_gate, m_w_ple_gate, v_w_ple_gate),
            ("w_pw", dw_pw, False, a // N_DEV, w_pw, m_w_pw, v_w_pw), ("w_ple", dw_ple, True, d // N_DEV, w_ple, m_w_ple, v_w_ple)]

    def pair_sums(tag, group):
        g_slicers = [(_col_slice if col else _row_slice)(sz) for _, _, col, sz, *_ in group]
        shard_shapes = [wt.shape[1:] for _, _, _, _, wt, _, _ in group]
        recv1 = _pair_exchange("grad_pair_exchange_" + tag, [g for _, g, *_ in group], g_slicers, shard_shapes)
        return [_pair_add("pair_add_" + nm, g, r1, col, core_idx) for (nm, g, col, *_), r1 in zip(group, recv1)]

    pairs_rest = pair_sums("rest", rest)
    dw_in, recv2_rest = _proj_bwd_w(xn, dproj7, comm=_chip_exchange_comm(pairs_rest))
    first = [("w_in", dw_in, True, ns, w_in, m_w_in, v_w_in)]
    pairs_in = pair_sums("w_in", first)
    dxn, recv2_in = _proj_bwd_x(dproj7, wf_in, comm=_chip_exchange_comm(pairs_in))
    grad_x, d_norm_in_g = _rms_bwd(dxn, x2, norm_in_g, dh1, "rms_in_bwd", False)

    dcol = _bias_grad(ds).reshape(heads, KB)
    onehot = jnp.asarray(_rel_index()[:, None] == np.arange(2 * MAX_REL + 1)[None, :], F32)
    d_rel = jnp.einsum("hj,jr->hr", dcol, onehot, precision=lax.Precision.HIGHEST)

    big_out = {}
    for (nm, _, _, _, wt, mt, vt), ps, r2 in zip(first + rest, pairs_in + pairs_rest, recv2_in + recv2_rest):
        g_, d_, m_, v_ = _adamw_big("adamw_" + nm, ps, r2, wt[0], mt[0], vt[0], chip_idx)
        big_out[nm] = tuple(q[None] for q in (g_, d_, m_, v_))

    small = [("norm_in_g", d_norm_in_g, norm_in_g, m_norm_in_g, v_norm_in_g),
             ("rel_table", d_rel, rel_table, m_rel_table, v_rel_table),
             ("b_dw", d_b_dw, b_dw, m_b_dw, v_b_dw),
             ("conv_ln_g", d_ln_g, conv_ln_g, m_conv_ln_g, v_conv_ln_g),
             ("conv_ln_b", d_ln_b, conv_ln_b, m_conv_ln_b, v_conv_ln_b),
             ("b_pw", cs_dxin[1], b_pw, m_b_pw, v_b_pw),
             ("attn_out_g", d_gains[0], attn_out_g, m_attn_out_g, v_attn_out_g),
             ("conv_out_g", d_gains[1], conv_out_g, m_conv_out_g, v_conv_out_g),
             ("ple_norm_g", d_ple_norm_g, ple_norm_g, m_ple_norm_g, v_ple_norm_g),
             ("b_ple_gate", d_b_gate, b_ple_gate, m_b_ple_gate, v_b_ple_gate),
             ("final_g", d_final_g, final_g, m_final_g, v_final_g)]
    g_parts = [g for _, g, *_ in small] + [d_w_dw[:CONV_K], loss_v[0, :1]]
    n_small = sum(int(np.prod(q.shape)) for q in g_parts)
    rows = -(-n_small // (8 * LANES)) * 8
    g_all = _all_reduce_small(_pack(g_parts, rows))
    shapes = [wt.shape for _, _, wt, _, _ in small] + [(CONV_K, a), (1,)]
    g_un = _unpack(g_all, shapes)
    loss = g_un[-1][0]
    g_w_dw = lax.dynamic_slice_in_dim(g_un[-2], me * (a // N_DEV), a // N_DEV, axis=1)[None]
    g_small = g_un[:len(small)] + [g_w_dw]
    rows2 = -(-(n_small - 1) // (8 * LANES)) * 8
    pk = lambda parts: _pack(parts, rows2)
    d_pk, m_pk, v_pk = _adamw_small(pk(g_small), pk([wt for _, _, wt, _, _ in small] + [w_dw]),
                                    pk([mt for _, _, _, mt, _ in small] + [m_w_dw]), pk([vt for _, _, _, _, vt in small] + [v_w_dw]))
    shapes2 = [wt.shape for _, _, wt, _, _ in small] + [w_dw.shape]
    names = [nm for nm, *_ in small] + ["w_dw"]
    small_out = {nm: (g, dd, mm, vv) for nm, g, dd, mm, vv in
                 zip(names, g_small, _unpack(d_pk, shapes2), _unpack(m_pk, shapes2), _unpack(v_pk, shapes2))}

    order = ["norm_in_g", "w_in", "rel_table", "w_dw", "b_dw", "conv_ln_g", "conv_ln_b", "w_pw", "b_pw", "attn_out_g",
             "conv_out_g", "w_out", "ple_norm_g", "w_ple_gate", "b_ple_gate", "w_ple", "final_g"]
    alias = {"w_ple_gate": "w_gate"}
    res = {nm: (big_out[alias.get(nm, nm)] if alias.get(nm, nm) in big_out else small_out[nm]) for nm in order}
    outs = [loss, grad_x[None]]
    for kind in range(4):
        outs += [res[nm][kind].reshape(w_shape) for nm, w_shape in zip(order, [
            norm_in_g.shape, w_in.shape, rel_table.shape, w_dw.shape, b_dw.shape, conv_ln_g.shape, conv_ln_b.shape, w_pw.shape,
            b_pw.shape, attn_out_g.shape, conv_out_g.shape, w_out.shape, ple_norm_g.shape, w_ple_gate.shape, b_ple_gate.shape,
            w_ple.shape, final_g.shape])]
    return tuple(outs)
```

```python
import functools

import numpy as np
import jax
import jax.numpy as jnp
from jax import lax
from jax.experimental import pallas as pl
from jax.experimental.pallas import tpu as pltpu

F32 = jnp.float32
BF16 = jnp.bfloat16
MESH = pl.DeviceIdType.MESH

CHUNK = 64
LEFT_CHUNKS = 8
HEAD_DIM = 128
MAX_REL = 256
CONV_K = 31
EPS = 1e-6
NEG_INF = -1e30
ADAM_LR = 0.001
ADAM_B1 = 0.9
ADAM_B2 = 0.999
ADAM_EPS = 1e-08
ADAM_WD = 0.01
ADAM_STEP = 10

N_DEV = 8
QB = CHUNK * LEFT_CHUNKS
KB = 2 * QB
HALO = 32
CONV_COLS = 256
CONV_ROWS = 64
LANES = 128
VMEM_CAP = 60 * 1024 * 1024
NT = (((1,), (1,)), ((), ()))
TN = (((0,), (0,)), ((), ()))
NN = (((1,), (0,)), ((), ()))


def _cparams(sem, est_bytes):
    return pltpu.CompilerParams(dimension_semantics=sem, vmem_limit_bytes=int(min(VMEM_CAP, max(32 << 20, 2 * est_bytes))))


def _nbytes(shape, dtype):
    return int(np.prod(shape)) * jnp.dtype(dtype).itemsize


def _row_block(rows, bytes_per_row, budget=12 << 20):
    tb = rows
    while tb > 16 and (tb * bytes_per_row * 2 > budget or rows % tb):
        tb //= 2
    assert rows % tb == 0
    return tb


def _sigmoid(x):
    return 1.0 / (1.0 + jnp.exp(-x))


def _cast_bf16(w, name):
    r, c = w.shape
    tb = _row_block(r, c * 6)

    def body(w_ref, o_ref):
        o_ref[...] = w_ref[...].astype(BF16)

    return pl.pallas_call(
        body, name=name, grid=(r // tb,),
        in_specs=[pl.BlockSpec((tb, c), lambda i: (i, 0))], out_specs=pl.BlockSpec((tb, c), lambda i: (i, 0)),
        out_shape=jax.ShapeDtypeStruct((r, c), BF16), compiler_params=_cparams(("parallel",), tb * c * 12))(w)


def _rms_fwd(x, g, name):
    t, d = x.shape
    tb = _row_block(t, d * 6)

    def body(x_ref, g_ref, o_ref):
        xv = x_ref[...]
        r = lax.rsqrt(jnp.mean(xv * xv, axis=-1, keepdims=True) + EPS)
        o_ref[...] = (xv * r * g_ref[...]).astype(BF16)

    return pl.pallas_call(
        body, name=name, grid=(t // tb,),
        in_specs=[pl.BlockSpec((tb, d), lambda i: (i, 0)), pl.BlockSpec((1, d), lambda i: (0, 0))],
        out_specs=pl.BlockSpec((tb, d), lambda i: (i, 0)),
        out_shape=jax.ShapeDtypeStruct((t, d), BF16), compiler_params=_cparams(("parallel",), tb * d * 12))(x, g)


def _rms_bwd(dy, x, g, resid, name, with_bf16):
    t, d = x.shape
    tb = _row_block(t, d * 18)

    def body(dy_ref, x_ref, g_ref, res_ref, *outs):
        i = pl.program_id(0)
        xv = x_ref[...]
        r = lax.rsqrt(jnp.mean(xv * xv, axis=-1, keepdims=True) + EPS)
        n = xv * r
        dyv = dy_ref[...].astype(F32)
        dn = dyv * g_ref[...]
        dx = res_ref[...] + r * (dn - n * jnp.mean(dn * n, axis=-1, keepdims=True))
        outs[0][...] = dx
        if with_bf16:
            outs[1][...] = dx.astype(BF16)
        dg_ref = outs[-1]
        part = jnp.sum(dyv * n, axis=0, keepdims=True)

        @pl.when(i == 0)
        def _():
            dg_ref[...] = part

        @pl.when(i > 0)
        def _():
            dg_ref[...] += part

    row = pl.BlockSpec((tb, d), lambda i: (i, 0))
    vec = pl.BlockSpec((1, d), lambda i: (0, 0))
    out_shape = [jax.ShapeDtypeStruct((t, d), F32)] + ([jax.ShapeDtypeStruct((t, d), BF16)] if with_bf16 else []) + [
        jax.ShapeDtypeStruct((1, d), F32)]
    out_specs = [row] + ([row] if with_bf16 else []) + [vec]
    return pl.pallas_call(
        body, name=name, grid=(t // tb,), in_specs=[row, row, vec, row], out_specs=out_specs, out_shape=out_shape,
        compiler_params=_cparams(("arbitrary",), tb * d * 36))(dy, x, g, resid)


def _final(h2, tgt, gate, pe, g):
    t, d = h2.shape
    tb = _row_block(t, d * 20)

    def body(h_ref, t_ref, gate_ref, pe_ref, g_ref, dh_ref, dpe_ref, dgl_ref, loss_ref, dfg_ref, dbg_ref):
        i = pl.program_id(0)
        hv = h_ref[...]
        r = lax.rsqrt(jnp.mean(hv * hv, axis=-1, keepdims=True) + EPS)
        n = hv * r
        gv = g_ref[...]
        err = n * gv - t_ref[...]
        loss_part = 0.5 * jnp.sum(jnp.mean(err * err, axis=-1, keepdims=True), axis=0, keepdims=True)
        dy = err * (1.0 / d)
        dn = dy * gv
        dh = r * (dn - n * jnp.mean(dn * n, axis=-1, keepdims=True))
        dh_ref[...] = dh
        gt = gate_ref[...].astype(F32)
        dpe_ref[...] = (dh * gt).astype(BF16)
        dgl = dh * pe_ref[...].astype(F32) * gt * (1.0 - gt)
        dgl_ref[...] = dgl.astype(BF16)
        dfg = jnp.sum(dy * n, axis=0, keepdims=True)
        dbg = jnp.sum(dgl, axis=0, keepdims=True)

        @pl.when(i == 0)
        def _():
            loss_ref[...] = jnp.broadcast_to(loss_part, loss_ref.shape)
            dfg_ref[...] = dfg
            dbg_ref[...] = dbg

        @pl.when(i > 0)
        def _():
            loss_ref[...] += jnp.broadcast_to(loss_part, loss_ref.shape)
            dfg_ref[...] += dfg
            dbg_ref[...] += dbg

    row = pl.BlockSpec((tb, d), lambda i: (i, 0))
    vec = pl.BlockSpec((1, d), lambda i: (0, 0))
    return pl.pallas_call(
        body, name="final_loss_bwd", grid=(t // tb,), in_specs=[row, row, row, row, vec],
        out_specs=[row, row, row, pl.BlockSpec((1, LANES), lambda i: (0, 0)), vec, vec],
        out_shape=[jax.ShapeDtypeStruct((t, d), F32), jax.ShapeDtypeStruct((t, d), BF16), jax.ShapeDtypeStruct((t, d), BF16),
                   jax.ShapeDtypeStruct((1, LANES), F32), jax.ShapeDtypeStruct((1, d), F32), jax.ShapeDtypeStruct((1, d), F32)],
        compiler_params=_cparams(("arbitrary",), tb * d * 40))(h2, tgt, gate, pe, g)


def _post_fwd(o, yc, proj7, gains):
    t, a = o.shape
    tb = _row_block(t, a * 14)
    nb = t // tb

    def body(o_ref, yc_ref, z_ref, g_ref, out_ref):
        s = pl.program_id(0)
        xin = jnp.where(s == 0, o_ref[...], yc_ref[...])
        r = lax.rsqrt(jnp.mean(xin * xin, axis=-1, keepdims=True) + EPS)
        z = z_ref[...].astype(F32)
        out_ref[...] = (xin * r * g_ref[...] * (z * _sigmoid(z))).astype(BF16)

    return pl.pallas_call(
        body, name="post_fwd", grid=(2, nb),
        in_specs=[pl.BlockSpec((tb, a), lambda s, i: (i * (1 - s) + (nb - 1) * s, 0)),
                  pl.BlockSpec((tb, a), lambda s, i: (i * s, 0)),
                  pl.BlockSpec((None, tb, a), lambda s, i: (3 + 3 * s, i, 0)),
                  pl.BlockSpec((None, 1, a), lambda s, i: (s, 0, 0))],
        out_specs=pl.BlockSpec((tb, a), lambda s, i: (i, s)),
        out_shape=jax.ShapeDtypeStruct((t, 2 * a), BF16),
        compiler_params=_cparams(("arbitrary", "arbitrary"), tb * a * 28))(o, yc, proj7, gains)


def _post_bwd(dycat, o, yc, proj7, gains):
    t, a = o.shape
    tb = _row_block(t, a * 24)
    nb = t // tb

    def body(dy_ref, o_ref, yc_ref, z_ref, g_ref, dz_ref, dx_ref, dg_ref, cs_ref):
        s = pl.program_id(0)
        i = pl.program_id(1)
        xin = jnp.where(s == 0, o_ref[...], yc_ref[...])
        r = lax.rsqrt(jnp.mean(xin * xin, axis=-1, keepdims=True) + EPS)
        n = xin * r
        z = z_ref[...].astype(F32)
        sg = _sigmoid(z)
        gv = g_ref[...]
        dy = dy_ref[...]
        dz_ref[...] = (dy * (n * gv) * (sg * (1.0 + z * (1.0 - sg)))).astype(BF16)
        dyn = dy * (z * sg)
        dn = dyn * gv
        dx = r * (dn - n * jnp.mean(dn * n, axis=-1, keepdims=True))
        dxb = dx.astype(BF16)
        dx_ref[...] = dxb
        dg = jnp.sum(dyn * n, axis=0, keepdims=True)
        cs = jnp.sum(dxb.astype(F32), axis=0, keepdims=True)

        @pl.when(i == 0)
        def _():
            dg_ref[...] = dg
            cs_ref[...] = cs

        @pl.when(i > 0)
        def _():
            dg_ref[...] += dg
            cs_ref[...] += cs

    vec = pl.BlockSpec((None, 1, a), lambda s, i: (s, 0, 0))
    return pl.pallas_call(
        body, name="post_bwd", grid=(2, nb),
        in_specs=[pl.BlockSpec((tb, a), lambda s, i: (i, s)),
                  pl.BlockSpec((tb, a), lambda s, i: (i * (1 - s) + (nb - 1) * s, 0)),
                  pl.BlockSpec((tb, a), lambda s, i: (i * s, 0)),
                  pl.BlockSpec((None, tb, a), lambda s, i: (3 + 3 * s, i, 0)),
                  vec],
        out_specs=[pl.BlockSpec((None, tb, a), lambda s, i: (3 + 3 * s, i, 0)),
                   pl.BlockSpec((None, tb, a), lambda s, i: (s, i, 0)), vec, vec],
        out_shape=[jax.ShapeDtypeStruct((7, t, a), BF16), jax.ShapeDtypeStruct((2, t, a), BF16),
                   jax.ShapeDtypeStruct((2, 1, a), F32), jax.ShapeDtypeStruct((2, 1, a), F32)],
        compiler_params=_cparams(("arbitrary", "arbitrary"), tb * a * 48))(dycat, o, yc, proj7, gains)


def _ln_silu_fwd(u1, g, b):
    t, c = u1.shape
    tb = _row_block(t, c * 6)

    def body(u_ref, g_ref, b_ref, o_ref):
        u = u_ref[...]
        mu = jnp.mean(u, axis=-1, keepdims=True)
        xc = u - mu
        rstd = lax.rsqrt(jnp.mean(xc * xc, axis=-1, keepdims=True) + EPS)
        u2 = xc * rstd * g_ref[...] + b_ref[...]
        o_ref[...] = (u2 * _sigmoid(u2)).astype(BF16)

    row = pl.BlockSpec((tb, c), lambda i: (i, 0))
    vec = pl.BlockSpec((1, c), lambda i: (0, 0))
    return pl.pallas_call(
        body, name="ln_silu_fwd", grid=(t // tb,), in_specs=[row, vec, vec], out_specs=row,
        out_shape=jax.ShapeDtypeStruct((t, c), BF16), compiler_params=_cparams(("parallel",), tb * c * 12))(u1, g, b)


def _ln_silu_bwd(du3, u1, g, b):
    t, c = u1.shape
    tb = _row_block(t, c * 12)

    def body(d_ref, u_ref, g_ref, b_ref, du_ref, dg_ref, db_ref, cs_ref):
        i = pl.program_id(0)
        u = u_ref[...]
        mu = jnp.mean(u, axis=-1, keepdims=True)
        xc = u - mu
        rstd = lax.rsqrt(jnp.mean(xc * xc, axis=-1, keepdims=True) + EPS)
        xh = xc * rstd
        gv = g_ref[...]
        u2 = xh * gv + b_ref[...]
        sg = _sigmoid(u2)
        du2 = d_ref[...] * (sg * (1.0 + u2 * (1.0 - sg)))
        dxh = du2 * gv
        du1 = rstd * (dxh - jnp.mean(dxh, axis=-1, keepdims=True) - xh * jnp.mean(dxh * xh, axis=-1, keepdims=True))
        du_ref[...] = du1
        dg = jnp.sum(du2 * xh, axis=0, keepdims=True)
        db = jnp.sum(du2, axis=0, keepdims=True)
        cs = jnp.sum(du1, axis=0, keepdims=True)

        @pl.when(i == 0)
        def _():
            dg_ref[...] = dg
            db_ref[...] = db
            cs_ref[...] = cs

        @pl.when(i > 0)
        def _():
            dg_ref[...] += dg
            db_ref[...] += db
            cs_ref[...] += cs

    row = pl.BlockSpec((tb, c), lambda i: (i, 0))
    vec = pl.BlockSpec((1, c), lambda i: (0, 0))
    vs = jax.ShapeDtypeStruct((1, c), F32)
    return pl.pallas_call(
        body, name="ln_silu_bwd", grid=(t // tb,), in_specs=[row, row, vec, vec], out_specs=[row, vec, vec, vec],
        out_shape=[jax.ShapeDtypeStruct((t, c), F32), vs, vs, vs],
        compiler_params=_cparams(("arbitrary",), tb * c * 24))(du3, u1, g, b)


def _conv_block(t):
    tb = 512
    while t % tb:
        tb //= 2
    assert tb >= HALO
    return tb


def _conv_fwd(proj7, w_dw, b_dw):
    _, t, c = proj7.shape
    tb = _conv_block(t)
    hb = tb // HALO
    cw = CONV_COLS

    def body(a_ref, g_ref, ah_ref, gh_ref, w_ref, b_ref, u1_ref, ext):
        i = pl.program_id(1)
        halo = ah_ref[...].astype(F32) * _sigmoid(gh_ref[...].astype(F32))
        ext[0:HALO, :] = jnp.where(i > 0, halo, 0.0)
        ext[HALO:HALO + tb, :] = a_ref[...].astype(F32) * _sigmoid(g_ref[...].astype(F32))
        bias = b_ref[...]
        for r0 in range(0, tb, CONV_ROWS):
            acc = jnp.broadcast_to(bias, (CONV_ROWS, cw))
            for k in range(CONV_K):
                off = r0 + HALO - (CONV_K - 1) + k
                acc = acc + ext[off:off + CONV_ROWS, :] * w_ref[k:k + 1, :]
            u1_ref[r0:r0 + CONV_ROWS, :] = acc

    blk = lambda seg: pl.BlockSpec((None, tb, cw), lambda j, i: (seg, i, j))
    halo = lambda seg: pl.BlockSpec((None, HALO, cw), lambda j, i: (seg, jnp.maximum(i * hb - 1, 0), j))
    return pl.pallas_call(
        body, name="conv_fwd", grid=(c // cw, t // tb),
        in_specs=[blk(4), blk(5), halo(4), halo(5), pl.BlockSpec((32, cw), lambda j, i: (0, j)),
                  pl.BlockSpec((1, cw), lambda j, i: (0, j))],
        out_specs=pl.BlockSpec((tb, cw), lambda j, i: (i, j)),
        out_shape=jax.ShapeDtypeStruct((t, c), F32),
        scratch_shapes=[pltpu.VMEM((tb + HALO, cw), F32)],
        compiler_params=_cparams(("parallel", "arbitrary"), tb * cw * 16))(proj7, proj7, proj7, proj7, w_dw, b_dw)


def _conv_bwd(dproj7, du1, proj7, w_dw):
    _, t, c = proj7.shape
    tb = _conv_block(t)
    hb = tb // HALO
    nb = t // tb
    cw = CONV_COLS

    def body(dp_ref, du_ref, duh_ref, a_ref, g_ref, ah_ref, gh_ref, w_ref, out_ref, dw_ref, dext, uext, dwp):
        del dp_ref
        i = pl.program_id(1)
        dext[0:tb, :] = du_ref[...]
        dext[tb:tb + HALO, :] = jnp.where(i < nb - 1, duh_ref[...], 0.0)
        halo = ah_ref[...].astype(F32) * _sigmoid(gh_ref[...].astype(F32))
        uext[0:HALO, :] = jnp.where(i > 0, halo, 0.0)
        for r0 in range(0, tb, CONV_ROWS):
            a_c = a_ref[r0:r0 + CONV_ROWS, :].astype(F32)
            s_c = _sigmoid(g_ref[r0:r0 + CONV_ROWS, :].astype(F32))
            uext[HALO + r0:HALO + r0 + CONV_ROWS, :] = a_c * s_c
            acc = jnp.zeros((CONV_ROWS, cw), F32)
            for k in range(CONV_K):
                off = r0 + (CONV_K - 1) - k
                acc = acc + dext[off:off + CONV_ROWS, :] * w_ref[k:k + 1, :]
            out_ref[0, r0:r0 + CONV_ROWS, :] = (acc * s_c).astype(BF16)
            out_ref[1, r0:r0 + CONV_ROWS, :] = (acc * a_c * s_c * (1.0 - s_c)).astype(BF16)
        for k in range(CONV_K):
            acc = jnp.zeros((8, cw), F32)
            for r0 in range(0, tb, CONV_ROWS):
                off = r0 + HALO - (CONV_K - 1) + k
                prod = dext[r0:r0 + CONV_ROWS, :] * uext[off:off + CONV_ROWS, :]
                acc = acc + jnp.sum(prod.reshape(CONV_ROWS // 8, 8, cw), axis=0)
            dwp[k:k + 1, :] = jnp.sum(acc, axis=0, keepdims=True)
        dwp[CONV_K:32, :] = jnp.zeros((32 - CONV_K, cw), F32)

        @pl.when(i == 0)
        def _():
            dw_ref[...] = dwp[...]

        @pl.when(i > 0)
        def _():
            dw_ref[...] += dwp[...]

    blk = lambda seg: pl.BlockSpec((None, tb, cw), lambda j, i: (seg, i, j))
    halo = lambda seg: pl.BlockSpec((None, HALO, cw), lambda j, i: (seg, jnp.maximum(i * hb - 1, 0), j))
    est = tb * cw * 40
    return pl.pallas_call(
        body, name="conv_bwd", grid=(c // cw, nb),
        in_specs=[pl.BlockSpec(memory_space=pl.ANY),
                  pl.BlockSpec((tb, cw), lambda j, i: (i, j)),
                  pl.BlockSpec((HALO, cw), lambda j, i: (jnp.minimum((i + 1) * hb, nb * hb - 1), j)),
                  blk(4), blk(5), halo(4), halo(5), pl.BlockSpec((32, cw), lambda j, i: (0, j))],
        out_specs=[pl.BlockSpec((2, tb, cw), lambda j, i: (2, i, j)), pl.BlockSpec((32, cw), lambda j, i: (0, j))],
        out_shape=[jax.ShapeDtypeStruct(dproj7.shape, BF16), jax.ShapeDtypeStruct((32, c), F32)],
        scratch_shapes=[pltpu.VMEM((tb + HALO, cw), F32), pltpu.VMEM((tb + HALO, cw), F32), pltpu.VMEM((32, cw), F32)],
        input_output_aliases={0: 0},
        compiler_params=_cparams(("parallel", "arbitrary"), est))(dproj7, du1, du1, proj7, proj7, proj7, proj7, w_dw)


class _Comm:
    def __init__(self, inputs, out_shape, sems, start, finish, aliased=False):
        self.inputs, self.out_shape, self.sems, self.start, self.finish, self.aliased = inputs, out_shape, sems, start, finish, aliased


def _pcall(body, *, name, grid, in_specs, out_specs, out_shape, operands, scratch=(), sem, est, comm=None, aliases=None):
    aliases = dict(aliases or {})
    if comm is None:
        return pl.pallas_call(body, name=name, grid=grid, in_specs=in_specs, out_specs=out_specs, out_shape=out_shape,
                              scratch_shapes=list(scratch), input_output_aliases=aliases,
                              compiler_params=_cparams(sem, est))(*operands)
    n_in, n_out, n_scr = len(in_specs), len(out_specs), len(scratch)
    n_ci, n_co, n_sem = len(comm.inputs), len(comm.out_shape), len(comm.sems)
    last_ids = [g - 1 for g in grid]

    def wrapped(*refs):
        ins, cin = refs[:n_in], refs[n_in:n_in + n_ci]
        outs = refs[n_in + n_ci:n_in + n_ci + n_out]
        cout = refs[n_in + n_ci + n_out:n_in + n_ci + n_out + n_co]
        scr = refs[n_in + n_ci + n_out + n_co:n_in + n_ci + n_out + n_co + n_scr]
        sems = refs[len(refs) - n_sem:]
        ids = [pl.program_id(k) for k in range(len(grid))]
        first = functools.reduce(jnp.logical_and, [i == 0 for i in ids])
        last = functools.reduce(jnp.logical_and, [i == l for i, l in zip(ids, last_ids)])

        @pl.when(first)
        def _():
            comm.start(cin, cout, sems)

        body(*ins, *outs, *scr)

        @pl.when(last)
        def _():
            comm.finish(cin, cout, sems)

    hbm = pl.BlockSpec(memory_space=pl.ANY)
    if comm.aliased:
        aliases.update({n_in + k: n_out + k for k in range(n_ci)})
    res = pl.pallas_call(
        wrapped, name=name, grid=grid, in_specs=[*in_specs, *([hbm] * n_ci)], out_specs=[*out_specs, *([hbm] * n_co)],
        out_shape=[*out_shape, *comm.out_shape], scratch_shapes=[*scratch, *comm.sems], input_output_aliases=aliases,
        compiler_params=_cparams(("arbitrary",) * len(grid), est))(*operands, *comm.inputs)
    return list(res[:n_out]), list(res[n_out:])


def _rel_index():
    j = np.arange(KB)
    key_minus_query = np.where(j < KB - CHUNK, j, j - KB) - QB
    return (np.clip(-key_minus_query, -MAX_REL, MAX_REL) + MAX_REL).astype(np.int32)


def _row_bits(shape):
    return lax.broadcasted_iota(jnp.int32, shape, 0)


def _bias_build(tvec):
    h = tvec.shape[0]

    def body(t_ref, o_ref):
        x = jnp.broadcast_to(t_ref[...], (QB, KB))
        row = _row_bits((QB, KB))
        for bit in range(QB.bit_length() - 1):
            x = jnp.where(((row >> bit) & 1) == 1, pltpu.roll(x, 1 << bit, 1), x)
        qc = row // CHUNK
        kc = lax.broadcasted_iota(jnp.int32, (QB, KB), 1) // CHUNK
        o_ref[...] = jnp.where((kc >= qc) & (kc <= qc + LEFT_CHUNKS), x, NEG_INF)

    return pl.pallas_call(
        body, name="bias_build", grid=(h,), in_specs=[pl.BlockSpec((None, 1, KB), lambda i: (i, 0, 0))],
        out_specs=pl.BlockSpec((None, QB, KB), lambda i: (i, 0, 0)), out_shape=jax.ShapeDtypeStruct((h, QB, KB), F32),
        compiler_params=_cparams(("parallel",), QB * KB * 16))(tvec)


def _bias_grad(ds):
    h = ds.shape[0]

    def body(d_ref, o_ref):
        x = d_ref[...]
        row = _row_bits((QB, KB))
        for bit in range(QB.bit_length() - 1):
            x = jnp.where(((row >> bit) & 1) == 1, pltpu.roll(x, KB - (1 << bit), 1), x)
        o_ref[...] = jnp.sum(x, axis=0, keepdims=True)

    return pl.pallas_call(
        body, name="bias_grad", grid=(h,), in_specs=[pl.BlockSpec((None, QB, KB), lambda i: (i, 0, 0))],
        out_specs=pl.BlockSpec((None, 1, KB), lambda i: (i, 0, 0)), out_shape=jax.ShapeDtypeStruct((h, 1, KB), F32),
        compiler_params=_cparams(("parallel",), QB * KB * 16))(ds)


def _attn_specs(a, nb, clamp):
    hpa = a // HEAD_DIM
    cur = (lambda i: jnp.minimum(i, nb - 1)) if clamp else (lambda i: i)
    prev = lambda i: jnp.maximum(cur(i) - 1, 0)
    q = pl.BlockSpec((None, QB, HEAD_DIM), lambda h, i: (0, cur(i), h))
    kp = pl.BlockSpec((None, QB, HEAD_DIM), lambda h, i: (1, prev(i), h))
    kc = pl.BlockSpec((None, QB, HEAD_DIM), lambda h, i: (1, cur(i), h))
    vp = pl.BlockSpec((None, QB, HEAD_DIM), lambda h, i: (2, prev(i), h))
    vc = pl.BlockSpec((None, QB, HEAD_DIM), lambda h, i: (2, cur(i), h))
    bias = pl.BlockSpec((None, QB, KB), lambda h, i: (h, 0, 0))
    del hpa
    return q, kp, kc, vp, vc, bias


def _scores(q, kp, kc, b_ref, first):
    scale = HEAD_DIM ** -0.5
    pen = jnp.where(first, NEG_INF, 0.0).astype(F32)
    sp = lax.dot_general(q, kp, NT, preferred_element_type=F32) * scale + b_ref[:, 0:QB] + pen
    sc = lax.dot_general(q, kc, NT, preferred_element_type=F32) * scale + b_ref[:, QB:KB]
    m = jnp.maximum(jnp.max(sp, axis=-1, keepdims=True), jnp.max(sc, axis=-1, keepdims=True))
    ep = jnp.exp(sp - m)
    ec = jnp.exp(sc - m)
    l = jnp.sum(ep, axis=-1, keepdims=True) + jnp.sum(ec, axis=-1, keepdims=True)
    return ep, ec, l


def _attn_fwd(proj7, bias, comm=None):
    _, t, a = proj7.shape
    nb = t // QB
    h = a // HEAD_DIM

    def body(q_ref, kp_ref, kc_ref, vp_ref, vc_ref, b_ref, o_ref):
        i = pl.program_id(1)
        ep, ec, l = _scores(q_ref[...], kp_ref[...], kc_ref[...], b_ref, i == 0)
        o = (jnp.dot(ep.astype(BF16), vp_ref[...], preferred_element_type=F32)
             + jnp.dot(ec.astype(BF16), vc_ref[...], preferred_element_type=F32))
        o_ref[...] = o / l

    res = _pcall(
        body, name="attn_fwd", grid=(h, nb), in_specs=list(_attn_specs(a, nb, False)),
        out_specs=[pl.BlockSpec((QB, HEAD_DIM), lambda hh, i: (i, hh))],
        out_shape=[jax.ShapeDtypeStruct((t, a), F32)], operands=(proj7, proj7, proj7, proj7, proj7, bias),
        sem=("parallel", "arbitrary"), est=QB * KB * 40, comm=comm)
    return res[0] if comm is None else (res[0][0], res[1])


def _attn_bwd(dproj7, proj7, do, bias):
    _, t, a = proj7.shape
    nb = t // QB
    h = a // HEAD_DIM
    scale = HEAD_DIM ** -0.5

    def body(dp_ref, q_ref, kp_ref, kc_ref, vp_ref, vc_ref, b_ref, do_ref, out_ref, ds_ref, dq_c, dk_c, dv_c):
        del dp_ref
        i = pl.program_id(1)

        @pl.when(i < nb)
        def _():
            q = q_ref[...]
            kp = kp_ref[...]
            kc = kc_ref[...]
            do = do_ref[...]
            ep, ec, l = _scores(q, kp, kc, b_ref, i == 0)
            inv = 1.0 / l
            pp = ep * inv
            pc = ec * inv
            dpp = lax.dot_general(do, vp_ref[...], NT, preferred_element_type=F32)
            dpc = lax.dot_general(do, vc_ref[...], NT, preferred_element_type=F32)
            delta = jnp.sum(pp * dpp, axis=-1, keepdims=True) + jnp.sum(pc * dpc, axis=-1, keepdims=True)
            dsp = pp * (dpp - delta)
            dsc = pc * (dpc - delta)

            @pl.when(i == 0)
            def _():
                ds_ref[:, 0:QB] = dsp
                ds_ref[:, QB:KB] = dsc

            @pl.when(i > 0)
            def _():
                ds_ref[:, 0:QB] += dsp
                ds_ref[:, QB:KB] += dsc

            dspb = (dsp * scale).astype(BF16)
            dscb = (dsc * scale).astype(BF16)
            dq = jnp.dot(dspb, kp, preferred_element_type=F32) + jnp.dot(dscb, kc, preferred_element_type=F32)
            dkp = lax.dot_general(dspb, q, TN, preferred_element_type=F32)
            dkc = lax.dot_general(dscb, q, TN, preferred_element_type=F32)
            dvp = lax.dot_general(pp.astype(BF16), do, TN, preferred_element_type=F32)
            dvc = lax.dot_general(pc.astype(BF16), do, TN, preferred_element_type=F32)

            @pl.when(i > 0)
            def _():
                out_ref[0] = dq_c[...].astype(BF16)
                out_ref[1] = (dk_c[...] + dkp).astype(BF16)
                out_ref[2] = (dv_c[...] + dvp).astype(BF16)

            dq_c[...] = dq
            dk_c[...] = dkc
            dv_c[...] = dvc

        @pl.when(i == nb)
        def _():
            out_ref[0] = dq_c[...].astype(BF16)
            out_ref[1] = dk_c[...].astype(BF16)
            out_ref[2] = dv_c[...].astype(BF16)

    q, kp, kc, vp, vc, bspec = _attn_specs(a, nb, True)
    return pl.pallas_call(
        body, name="attn_bwd", grid=(h, nb + 1),
        in_specs=[pl.BlockSpec(memory_space=pl.ANY), q, kp, kc, vp, vc, bspec,
                  pl.BlockSpec((None, QB, HEAD_DIM), lambda hh, i: (0, jnp.minimum(i, nb - 1), hh))],
        out_specs=[pl.BlockSpec((3, QB, HEAD_DIM), lambda hh, i: (0, jnp.maximum(i - 1, 0), hh)),
                   pl.BlockSpec((None, QB, KB), lambda hh, i: (hh, 0, 0))],
        out_shape=[jax.ShapeDtypeStruct(dproj7.shape, BF16), jax.ShapeDtypeStruct((h, QB, KB), F32)],
        scratch_shapes=[pltpu.VMEM((QB, HEAD_DIM), F32)] * 3,
        input_output_aliases={0: 0},
        compiler_params=_cparams(("parallel", "arbitrary"), QB * KB * 60))(dproj7, proj7, proj7, proj7, proj7, proj7, bias, do)


def _tile(n, pref):
    t = min(n, pref)
    while n % t:
        t //= 2
    return t


def _mm(name, a, b, dn, grid, a_spec, b_spec, acc_shape, out_shape, out_specs, epilogue, extra=(), extra_specs=(), est=0,
        comm=None):
    nk = grid[2]
    n_extra = len(extra)
    single = not isinstance(out_shape, (list, tuple))
    if single:
        out_shape, out_specs = [out_shape], [out_specs]

    def body(*refs):
        a_ref, b_ref = refs[0], refs[1]
        ex = refs[2:2 + n_extra]
        outs = refs[2 + n_extra:-1]
        acc = refs[-1]
        k = pl.program_id(2)
        prod = lax.dot_general(a_ref[...].astype(BF16), b_ref[...].astype(BF16), dn, preferred_element_type=F32)

        @pl.when(k == 0)
        def _():
            acc[...] = prod

        @pl.when(k > 0)
        def _():
            acc[...] += prod

        @pl.when(k == nk - 1)
        def _():
            epilogue(acc[...], ex, outs)

    res = _pcall(body, name=name, grid=grid, in_specs=[a_spec, b_spec, *extra_specs], out_specs=list(out_specs),
                 out_shape=list(out_shape), operands=(a, b, *extra), scratch=[pltpu.VMEM(acc_shape, F32)],
                 sem=("parallel", "parallel", "arbitrary"), est=est, comm=comm)
    if comm is None:
        return res[0] if single else res
    return (res[0][0] if single else res[0]), res[1]


def _store(dtype):
    def ep(acc, ex, outs):
        outs[0][...] = acc.astype(dtype)
    return ep


def _mm_est(tm, tn, tk, out_bytes):
    return 2 * (tm * tk * 4 + tk * tn * 2) + tm * tn * 4 + 2 * tm * tn * out_bytes


def _mm_nn(name, a, b, out_dtype, bias=None, resid=None):
    m, k = a.shape
    n = b.shape[1]
    tm, tn, tk = _tile(m, 1024), _tile(n, 1024), _tile(k, 2048)
    extra, especs = [], []
    if bias is not None:
        extra.append(bias)
        especs.append(pl.BlockSpec((1, tn), lambda i, j, kk: (0, j)))
    if resid is not None:
        extra.append(resid)
        especs.append(pl.BlockSpec((tm, tn), lambda i, j, kk: (i, j)))

    def ep(acc, ex, outs):
        for r in ex:
            acc = acc + r[...]
        outs[0][...] = acc.astype(out_dtype)

    return _mm(name, a, b, NN, (m // tm, n // tn, k // tk),
               pl.BlockSpec((tm, tk), lambda i, j, kk: (i, kk)), pl.BlockSpec((tk, tn), lambda i, j, kk: (kk, j)),
               (tm, tn), jax.ShapeDtypeStruct((m, n), out_dtype), pl.BlockSpec((tm, tn), lambda i, j, kk: (i, j)),
               ep, extra, especs, _mm_est(tm, tn, tk, 12))


def _mm_nt(name, a, b, out_dtype):
    m, k = a.shape
    n = b.shape[0]
    tm, tn, tk = _tile(m, 1024), _tile(n, 1024), _tile(k, 2048)
    return _mm(name, a, b, NT, (m // tm, n // tn, k // tk),
               pl.BlockSpec((tm, tk), lambda i, j, kk: (i, kk)), pl.BlockSpec((tn, tk), lambda i, j, kk: (j, kk)),
               (tm, tn), jax.ShapeDtypeStruct((m, n), out_dtype), pl.BlockSpec((tm, tn), lambda i, j, kk: (i, j)),
               _store(out_dtype), est=_mm_est(tm, tn, tk, 4))


def _mm_tn(name, a, b, out_dtype):
    k, m = a.shape
    n = b.shape[1]
    tm, tn, tk = _tile(m, 1024), _tile(n, 1024), _tile(k, 2048)
    return _mm(name, a, b, TN, (m // tm, n // tn, k // tk),
               pl.BlockSpec((tk, tm), lambda i, j, kk: (kk, i)), pl.BlockSpec((tk, tn), lambda i, j, kk: (kk, j)),
               (tm, tn), jax.ShapeDtypeStruct((m, n), out_dtype), pl.BlockSpec((tm, tn), lambda i, j, kk: (i, j)),
               _store(out_dtype), est=_mm_est(tm, tn, tk, 4))


def _proj_fwd(xn, w_in, seg, comm=None):
    t, d = xn.shape
    ncol = w_in.shape[1]
    tm, tn, tk = _tile(t, 1024), _tile(seg, 1024), _tile(d, 2048)
    per = seg // tn
    return _mm("proj_fwd", xn, w_in, NN, (t // tm, ncol // tn, d // tk),
               pl.BlockSpec((tm, tk), lambda i, j, kk: (i, kk)), pl.BlockSpec((tk, tn), lambda i, j, kk: (kk, j)),
               (tm, tn), jax.ShapeDtypeStruct((ncol // seg, t, seg), BF16),
               pl.BlockSpec((None, tm, tn), lambda i, j, kk: (j // per, i, j % per)),
               _store(BF16), est=_mm_est(tm, tn, tk, 2), comm=comm)


def _proj_bwd_x(dproj7, w_in, comm=None):
    nseg, t, seg = dproj7.shape
    d = w_in.shape[0]
    tm, tn, tk = _tile(t, 1024), _tile(d, 1024), _tile(seg, 2048)
    per = seg // tk
    return _mm("proj_bwd_x", dproj7, w_in, NT, (t // tm, d // tn, nseg * per),
               pl.BlockSpec((None, tm, tk), lambda i, j, kk: (kk // per, i, kk % per)),
               pl.BlockSpec((tn, tk), lambda i, j, kk: (j, kk)),
               (tm, tn), jax.ShapeDtypeStruct((t, d), F32), pl.BlockSpec((tm, tn), lambda i, j, kk: (i, j)),
               _store(F32), est=_mm_est(tm, tn, tk, 4), comm=comm)


def _proj_bwd_w(xn, dproj7, comm=None):
    nseg, t, seg = dproj7.shape
    d = xn.shape[1]
    tm, tn, tk = _tile(d, 1024), _tile(seg, 1024), _tile(t, 2048)
    per = seg // tn
    return _mm("proj_bwd_w", xn, dproj7, TN, (d // tm, nseg * per, t // tk),
               pl.BlockSpec((tk, tm), lambda i, j, kk: (kk, i)),
               pl.BlockSpec((None, tk, tn), lambda i, j, kk: (j // per, kk, j % per)),
               (tm, tn), jax.ShapeDtypeStruct((d, nseg * seg), BF16), pl.BlockSpec((tm, tn), lambda i, j, kk: (i, j)),
               _store(BF16), est=_mm_est(tm, tn, tk, 2), comm=comm)


def _gate_fwd(hn, w_g, b_g, h1, p, w_ple):
    t, d = hn.shape
    pd = p.shape[1]
    tm, tn, tk = _tile(t, 1024), _tile(d, 1024), _tile(d, 1024)

    def ep(acc, ex, outs):
        b_ref, h1_ref, p_ref, wp_ref = ex
        gate = _sigmoid(acc + b_ref[...])
        pe = jnp.dot(p_ref[...].astype(BF16), wp_ref[...], preferred_element_type=F32)
        outs[0][...] = h1_ref[...] + gate * pe
        outs[1][...] = gate.astype(BF16)
        outs[2][...] = pe.astype(BF16)

    tile = pl.BlockSpec((tm, tn), lambda i, j, kk: (i, j))
    return _mm("gate_fwd", hn, w_g, NN, (t // tm, d // tn, d // tk),
               pl.BlockSpec((tm, tk), lambda i, j, kk: (i, kk)), pl.BlockSpec((tk, tn), lambda i, j, kk: (kk, j)),
               (tm, tn),
               [jax.ShapeDtypeStruct((t, d), F32), jax.ShapeDtypeStruct((t, d), BF16), jax.ShapeDtypeStruct((t, d), BF16)],
               [tile, tile, tile], ep, [b_g, h1, p, w_ple],
               [pl.BlockSpec((1, tn), lambda i, j, kk: (0, j)), tile, pl.BlockSpec((tm, pd), lambda i, j, kk: (i, 0)),
                pl.BlockSpec((pd, tn), lambda i, j, kk: (0, j))],
               _mm_est(tm, tn, tk, 16) + 2 * (tm * pd * 4 + pd * tn * 2))


def _coords():
    return lax.axis_index("x"), lax.axis_index("y"), lax.axis_index("c")


def _row_slice(rows):
    return lambda ref, o: ref.at[pl.ds(pl.multiple_of(o * rows, rows), rows), :]


def _col_slice(cols):
    return lambda ref, o: ref.at[:, pl.ds(pl.multiple_of(o * cols, cols), cols)]


def _all_gather(shards, slicers, full_shapes):
    n = len(shards)

    def body(*refs):
        srcs, outs = refs[:n], refs[n:2 * n]
        send_sems, recv_sems, local_sems = refs[2 * n:]
        x, y, c = _coords()
        me, sibling = (x, y, c), (x, y, 1 - c)
        chips = [(1 - x, y), (x, 1 - y), (1 - x, 1 - y)]

        def copy(w, k, block, to, src=None):
            dst = slicers[w](outs[w], 4 * block[0] + 2 * block[1] + block[2])
            return pltpu.make_async_remote_copy(
                src_ref=dst if src is None else src, dst_ref=dst, send_sem=send_sems.at[w, k], recv_sem=recv_sems.at[w, k],
                device_id=to, device_id_type=MESH)

        mine = [pltpu.make_async_copy(srcs[w], slicers[w](outs[w], 4 * x + 2 * y + c), local_sems.at[w]) for w in range(n)]
        for cp in mine:
            cp.start()
        first = []
        for w in range(n):
            first.append(copy(w, 0, me, sibling, src=srcs[w]))
            first += [copy(w, 1 + j, me, (*chip, c), src=srcs[w]) for j, chip in enumerate(chips)]
        for cp in first:
            cp.start()
        passed = []
        for j, chip in enumerate(chips):
            for w in range(n):
                copy(w, 1 + j, (*chip, c), me).wait_recv()
                fwd = copy(w, 4 + j, (*chip, c), sibling)
                fwd.start()
                passed.append(fwd)
        for w in range(n):
            copy(w, 0, sibling, me).wait_recv()
            for j, chip in enumerate(chips):
                copy(w, 4 + j, (*chip, 1 - c), me).wait_recv()
        for cp in first + passed:
            cp.wait_send()
        for cp in mine:
            cp.wait()

    hbm = pl.BlockSpec(memory_space=pl.ANY)
    return pl.pallas_call(
        body, name="all_gather_weights",
        out_shape=[jax.ShapeDtypeStruct(s, a.dtype) for s, a in zip(full_shapes, shards)],
        in_specs=[hbm] * n, out_specs=[hbm] * n,
        scratch_shapes=[pltpu.SemaphoreType.DMA((n, 7)), pltpu.SemaphoreType.DMA((n, 7)), pltpu.SemaphoreType.DMA((n,))],
        compiler_params=pltpu.CompilerParams(has_side_effects=True))(*shards)


def _gather_ici_comm(shards, slicers, full_shapes):
    n = len(shards)

    def descs(cin, cout, sems):
        send_sems, recv_sems, local_sems = sems
        x, y, c = _coords()
        me, sibling = (x, y, c), (x, y, 1 - c)
        chips = [(1 - x, y), (x, 1 - y), (1 - x, 1 - y)]

        def copy(w, k, block, to, src=None):
            dst = slicers[w](cout[w], 4 * block[0] + 2 * block[1] + block[2])
            return pltpu.make_async_remote_copy(
                src_ref=dst if src is None else src, dst_ref=dst, send_sem=send_sems.at[w, k], recv_sem=recv_sems.at[w, k],
                device_id=to, device_id_type=MESH)

        mine = [pltpu.make_async_copy(cin[w], slicers[w](cout[w], 4 * x + 2 * y + c), local_sems.at[w]) for w in range(n)]
        sends, recvs = [], []
        for w in range(n):
            sends.append(copy(w, 0, me, sibling, src=cin[w]))
            recvs.append(copy(w, 0, sibling, me))
            for j, chip in enumerate(chips):
                sends.append(copy(w, 1 + j, me, (*chip, c), src=cin[w]))
                recvs.append(copy(w, 1 + j, (*chip, c), me))
        return mine, sends, recvs

    def start(cin, cout, sems):
        mine, sends, _ = descs(cin, cout, sems)
        for cp in mine + sends:
            cp.start()

    def finish(cin, cout, sems):
        mine, sends, recvs = descs(cin, cout, sems)
        for cp in recvs:
            cp.wait_recv()
        for cp in sends:
            cp.wait_send()
        for cp in mine:
            cp.wait()

    return _Comm(list(shards), [jax.ShapeDtypeStruct(s, a.dtype) for s, a in zip(full_shapes, shards)],
                 [pltpu.SemaphoreType.DMA((n, 4)), pltpu.SemaphoreType.DMA((n, 4)), pltpu.SemaphoreType.DMA((n,))], start, finish)


def _gather_d2d_comm(fulls, slicers):
    n = len(fulls)

    def descs(cout, sems):
        send_sems, recv_sems = sems
        x, y, c = _coords()
        sibling = (x, y, 1 - c)
        chips = [(1 - x, y), (x, 1 - y), (1 - x, 1 - y)]
        sends, recvs = [], []
        for w in range(n):
            for j, (px, py) in enumerate(chips):
                for core, out in ((c, sends), (1 - c, recvs)):
                    blk = slicers[w](cout[w], 4 * px + 2 * py + core)
                    out.append(pltpu.make_async_remote_copy(
                        src_ref=blk, dst_ref=blk, send_sem=send_sems.at[w, j], recv_sem=recv_sems.at[w, j],
                        device_id=sibling, device_id_type=MESH))
        return sends, recvs

    def start(cin, cout, sems):
        for cp in descs(cout, sems)[0]:
            cp.start()

    def finish(cin, cout, sems):
        sends, recvs = descs(cout, sems)
        for cp in recvs:
            cp.wait_recv()
        for cp in sends:
            cp.wait_send()

    return _Comm(list(fulls), [jax.ShapeDtypeStruct(f.shape, f.dtype) for f in fulls],
                 [pltpu.SemaphoreType.DMA((n, 3)), pltpu.SemaphoreType.DMA((n, 3))], start, finish, aliased=True)


def _chip_exchange_comm(pairsums):
    n = len(pairsums)

    def descs(cin, cout, sems):
        send_sems, recv_sems = sems
        x, y, c = _coords()
        copies = []
        for w in range(n):
            for r in (1, 2, 3):
                px, py = x ^ (r >> 1), y ^ (r & 1)
                copies.append(pltpu.make_async_remote_copy(
                    src_ref=cin[w].at[2 * px + py], dst_ref=cout[w].at[r - 1],
                    send_sem=send_sems.at[w, r - 1], recv_sem=recv_sems.at[w, r - 1], device_id=(px, py, c), device_id_type=MESH))
        return copies

    def start(cin, cout, sems):
        for cp in descs(cin, cout, sems):
            cp.start()

    def finish(cin, cout, sems):
        copies = descs(cin, cout, sems)
        for cp in copies:
            cp.wait_recv()
        for cp in copies:
            cp.wait_send()

    return _Comm(list(pairsums), [jax.ShapeDtypeStruct((3, *p.shape[1:]), p.dtype) for p in pairsums],
                 [pltpu.SemaphoreType.DMA((n, 3)), pltpu.SemaphoreType.DMA((n, 3))], start, finish)


def _pair_exchange(name, grads, slicers, shard_shapes):
    n = len(grads)

    def body(*refs):
        srcs, outs = refs[:n], refs[n:2 * n]
        send_sems, recv_sems = refs[2 * n:]
        x, y, c = _coords()
        sibling = (x, y, 1 - c)
        copies = []
        for w in range(n):
            for j in range(4):
                copies.append(pltpu.make_async_remote_copy(
                    src_ref=slicers[w](srcs[w], 2 * j + (1 - c)), dst_ref=outs[w].at[j],
                    send_sem=send_sems.at[w, j], recv_sem=recv_sems.at[w, j], device_id=sibling, device_id_type=MESH))
        for cp in copies:
            cp.start()
        for cp in copies:
            cp.wait_recv()
        for cp in copies:
            cp.wait_send()

    hbm = pl.BlockSpec(memory_space=pl.ANY)
    return pl.pallas_call(
        body, name=name,
        out_shape=[jax.ShapeDtypeStruct((4, *s), g.dtype) for s, g in zip(shard_shapes, grads)],
        in_specs=[hbm] * n, out_specs=[hbm] * n,
        scratch_shapes=[pltpu.SemaphoreType.DMA((n, 4)), pltpu.SemaphoreType.DMA((n, 4))],
        compiler_params=pltpu.CompilerParams(has_side_effects=True))(*grads)


def _pair_add(name, grad, recv, col_sharded, cidx):
    _, r, cc = recv.shape
    tr = _row_block(r, cc * 6, budget=8 << 20)
    nr = r // tr
    if col_sharded:
        g_spec = pl.BlockSpec((tr, cc), lambda j, i, s: (i, 2 * j + s[0]))
    else:
        g_spec = pl.BlockSpec((tr, cc), lambda j, i, s: ((2 * j + s[0]) * nr + i, 0))

    def body(s_ref, g_ref, r_ref, o_ref):
        del s_ref
        o_ref[...] = (g_ref[...].astype(F32) + r_ref[...].astype(F32)).astype(BF16)

    return pl.pallas_call(
        body, name=name,
        grid_spec=pltpu.PrefetchScalarGridSpec(
            num_scalar_prefetch=1, grid=(4, nr),
            in_specs=[g_spec, pl.BlockSpec((None, tr, cc), lambda j, i, s: (j, i, 0))],
            out_specs=pl.BlockSpec((None, tr, cc), lambda j, i, s: (j, i, 0))),
        out_shape=jax.ShapeDtypeStruct(recv.shape, BF16),
        compiler_params=_cparams(("parallel", "parallel"), tr * cc * 12))(cidx, grad, recv)


def _all_reduce_small(vec):
    r = vec.shape[0]

    def body(v_ref, o_ref, gath, send_sems, recv_sems):
        x, y, c = _coords()
        me = 4 * x + 2 * y + c
        copies = []
        for rel in range(1, N_DEV):
            peer = (x ^ (rel >> 2), y ^ ((rel >> 1) & 1), c ^ (rel & 1))
            copies.append(pltpu.make_async_remote_copy(
                src_ref=v_ref, dst_ref=gath.at[me], send_sem=send_sems.at[rel - 1], recv_sem=recv_sems.at[rel - 1],
                device_id=peer, device_id_type=MESH))
        for cp in copies:
            cp.start()
        gath[me] = v_ref[...]
        for rel in range(1, N_DEV):
            src = 4 * (x ^ (rel >> 2)) + 2 * (y ^ ((rel >> 1) & 1)) + (c ^ (rel & 1))
            pltpu.make_async_remote_copy(
                src_ref=v_ref, dst_ref=gath.at[src], send_sem=send_sems.at[rel - 1], recv_sem=recv_sems.at[rel - 1],
                device_id=(x, y, c), device_id_type=MESH).wait_recv()
        for cp in copies:
            cp.wait_send()
        acc = gath[0]
        for d in range(1, N_DEV):
            acc = acc + gath[d]
        o_ref[...] = acc

    vm = pl.BlockSpec(memory_space=pltpu.VMEM)
    return pl.pallas_call(
        body, name="all_reduce_small", out_shape=jax.ShapeDtypeStruct(vec.shape, F32), in_specs=[vm], out_specs=vm,
        scratch_shapes=[pltpu.VMEM((N_DEV, r, LANES), F32), pltpu.SemaphoreType.DMA((N_DEV - 1,)),
                        pltpu.SemaphoreType.DMA((N_DEV - 1,))],
        compiler_params=pltpu.CompilerParams(has_side_effects=True, vmem_limit_bytes=int(min(VMEM_CAP, 32 * r * LANES * 4 + (16 << 20)))))(vec)


def _adamw_math(w, g, m, v):
    m = ADAM_B1 * m + (1.0 - ADAM_B1) * g
    v = ADAM_B2 * v + (1.0 - ADAM_B2) * (g * g)
    m_hat = m / (1.0 - ADAM_B1 ** ADAM_STEP)
    v_hat = v / (1.0 - ADAM_B2 ** ADAM_STEP)
    delta = -ADAM_LR * (m_hat / (jnp.sqrt(v_hat) + ADAM_EPS) + ADAM_WD * w)
    return delta, m, v


def _adamw_big(name, pairsum, recv, w, m, v, chip_idx):
    r, cc = w.shape
    tr = _row_block(r, cc * 36, budget=16 << 20)

    def body(s_ref, p_ref, r_ref, w_ref, m_ref, v_ref, g_out, d_out, m_out, v_out):
        del s_ref
        g = p_ref[...].astype(F32)
        for k in range(3):
            g = g + r_ref[k].astype(F32)
        delta, m2, v2 = _adamw_math(w_ref[...], g, m_ref[...], v_ref[...])
        g_out[...] = g
        d_out[...] = delta
        m_out[...] = m2
        v_out[...] = v2

    tile = pl.BlockSpec((tr, cc), lambda i, s: (i, 0))
    sds = jax.ShapeDtypeStruct((r, cc), F32)
    return pl.pallas_call(
        body, name=name,
        grid_spec=pltpu.PrefetchScalarGridSpec(
            num_scalar_prefetch=1, grid=(r // tr,),
            in_specs=[pl.BlockSpec((None, tr, cc), lambda i, s: (s[0], i, 0)), pl.BlockSpec((3, tr, cc), lambda i, s: (0, i, 0)),
                      tile, tile, tile],
            out_specs=[tile, tile, tile, tile]),
        out_shape=[sds, sds, sds, sds],
        compiler_params=_cparams(("parallel",), tr * cc * 72))(chip_idx, pairsum, recv, w, m, v)


def _adamw_small(g, w, m, v):
    r = g.shape[0]

    def body(g_ref, w_ref, m_ref, v_ref, d_out, m_out, v_out):
        delta, m2, v2 = _adamw_math(w_ref[...], g_ref[...], m_ref[...], v_ref[...])
        d_out[...] = delta
        m_out[...] = m2
        v_out[...] = v2

    vm = pl.BlockSpec(memory_space=pltpu.VMEM)
    sds = jax.ShapeDtypeStruct((r, LANES), F32)
    return pl.pallas_call(body, name="adamw_small", out_shape=[sds, sds, sds], in_specs=[vm] * 4, out_specs=[vm] * 3,
                          compiler_params=_cparams(None, r * LANES * 28))(g, w, m, v)


def _pack(parts, rows):
    flat = jnp.concatenate([q.reshape(-1).astype(F32) for q in parts])
    return jnp.pad(flat, (0, rows * LANES - flat.shape[0])).reshape(rows, LANES)


def _unpack(packed, shapes):
    flat = packed.reshape(-1)
    out, off = [], 0
    for s in shapes:
        n = int(np.prod(s))
        out.append(flat[off:off + n].reshape(s))
        off += n
    return out


def kernel(x, p, norm_in_g, w_in, rel_table, w_dw, b_dw, conv_ln_g, conv_ln_b, w_pw, b_pw, attn_out_g, conv_out_g, w_out, ple_norm_g, w_ple_gate, b_ple_gate, w_ple, final_g, loss_target, m_norm_in_g, m_w_in, m_rel_table, m_w_dw, m_b_dw, m_conv_ln_g, m_conv_ln_b, m_w_pw, m_b_pw, m_attn_out_g, m_conv_out_g, m_w_out, m_ple_norm_g, m_w_ple_gate, m_b_ple_gate, m_w_ple, m_final_g, v_norm_in_g, v_w_in, v_rel_table, v_w_dw, v_b_dw, v_conv_ln_g, v_conv_ln_b, v_w_pw, v_b_pw, v_attn_out_g, v_conv_out_g, v_w_out, v_ple_norm_g, v_w_ple_gate, v_b_ple_gate, v_w_ple, v_final_g):
    t, d = x.shape[1], x.shape[2]
    a = d // 2
    ncol = 7 * a
    ns = ncol // N_DEV
    pd = p.shape[-1]
    heads = a // HEAD_DIM
    assert x.shape[0] == 1 and t % QB == 0 and ns % LANES == 0 and w_in.shape == (1, d, ns)
    cx, cy, cc_ = _coords()
    me = 4 * cx + 2 * cy + cc_
    core_idx = jnp.reshape(cc_, (1,)).astype(jnp.int32)
    chip_idx = jnp.reshape(2 * cx + cy, (1,)).astype(jnp.int32)

    x2, tgt, p2 = x[0], loss_target[0], p[0, 0]
    final_g2 = final_g.reshape(1, d)

    w_dw_pad = jnp.pad(w_dw[0], ((0, 32 - CONV_K), (0, 0)))
    (wf_in,) = _all_gather([_cast_bf16(w_in[0], "cast_w_in")], [_col_slice(ns)], [(d, ncol)])
    shards = [_cast_bf16(w_out[0], "cast_w_out"), _cast_bf16(w_ple_gate[0], "cast_w_gate"),
              _cast_bf16(w_pw[0], "cast_w_pw"), _cast_bf16(w_ple[0], "cast_w_ple"), w_dw_pad]
    slicers = [_row_slice(d // N_DEV), _row_slice(d // N_DEV), _row_slice(a // N_DEV), _col_slice(d // N_DEV), _col_slice(a // N_DEV)]
    full_shapes = [(d, d), (d, d), (a, a), (pd, d), (32, a)]

    xn = _rms_fwd(x2, norm_in_g, "rms_in_fwd")
    proj7, fulls = _proj_fwd(xn, wf_in, a, comm=_gather_ici_comm(shards, slicers, full_shapes))
    tvec = rel_table[0][:, _rel_index()].reshape(heads, 1, KB)
    bias = _bias_build(tvec)
    o, (wf_out, wf_gate, wf_pw, wf_ple, wf_dw) = _attn_fwd(proj7, bias, comm=_gather_d2d_comm(fulls, slicers))
    u1 = _conv_fwd(proj7, wf_dw, b_dw)
    u3 = _ln_silu_fwd(u1, conv_ln_g, conv_ln_b)
    yc = _mm_nn("pw_fwd", u3, wf_pw, F32, bias=b_pw)
    gains = jnp.stack([attn_out_g, conv_out_g])
    ycat = _post_fwd(o, yc, proj7, gains)
    h1 = _mm_nn("out_fwd", ycat, wf_out, F32, resid=x2)
    hn = _rms_fwd(h1, ple_norm_g, "rms_ple_fwd")
    h2, gate, pe = _gate_fwd(hn, wf_gate, b_ple_gate, h1, p2, wf_ple)

    dh2, dpe, dgl, loss_v, d_final_g, d_b_gate = _final(h2, tgt, gate, pe, final_g2)
    dhn = _mm_nt("gate_bwd_x", dgl, wf_gate, F32)
    dw_gate = _mm_tn("gate_bwd_w", hn, dgl, BF16)
    dw_ple = _mm_tn("ple_bwd_w", p2, dpe, BF16)
    dh1, dh1b, d_ple_norm_g = _rms_bwd(dhn, h1, ple_norm_g, dh2, "rms_ple_bwd", True)
    dycat = _mm_nt("out_bwd_x", dh1b, wf_out, F32)
    dw_out = _mm_tn("out_bwd_w", ycat, dh1b, BF16)
    dproj7, dxin, d_gains, cs_dxin = _post_bwd(dycat, o, yc, proj7, gains)
    dproj7, ds = _attn_bwd(dproj7, proj7, dxin, bias)
    dyc = dxin[1]
    du3 = _mm_nt("pw_bwd_x", dyc, wf_pw, F32)
    dw_pw = _mm_tn("pw_bwd_w", u3, dyc, BF16)
    du1, d_ln_g, d_ln_b, d_b_dw = _ln_silu_bwd(du3, u1, conv_ln_g, conv_ln_b)
    dproj7, d_w_dw = _conv_bwd(dproj7, du1, proj7, wf_dw)

    rest = [("w_out", dw_out, False, d // N_DEV, w_out, m_w_out, v_w_out),
            ("w_gate", dw_gate, False, d // N_DEV, w_ple_gate, m_w_ple_gate, v_w_ple_gate),
            ("w_pw", dw_pw, False, a // N_DEV, w_pw, m_w_pw, v_w_pw), ("w_ple", dw_ple, True, d // N_DEV, w_ple, m_w_ple, v_w_ple)]

    def pair_sums(tag, group):
        g_slicers = [(_col_slice if col else _row_slice)(sz) for _, _, col, sz, *_ in group]
        shard_shapes = [wt.shape[1:] for _, _, _, _, wt, _, _ in group]
        recv1 = _pair_exchange("grad_pair_exchange_" + tag, [g for _, g, *_ in group], g_slicers, shard_shapes)
        return [_pair_add("pair_add_" + nm, g, r1, col, core_idx) for (nm, g, col, *_), r1 in zip(group, recv1)]

    pairs_rest = pair_sums("rest", rest)
    dw_in, recv2_rest = _proj_bwd_w(xn, dproj7, comm=_chip_exchange_comm(pairs_rest))
    first = [("w_in", dw_in, True, ns, w_in, m_w_in, v_w_in)]
    pairs_in = pair_sums("w_in", first)
    dxn, recv2_in = _proj_bwd_x(dproj7, wf_in, comm=_chip_exchange_comm(pairs_in))
    grad_x, d_norm_in_g = _rms_bwd(dxn, x2, norm_in_g, dh1, "rms_in_bwd", False)

    dcol = _bias_grad(ds).reshape(heads, KB)
    onehot = jnp.asarray(_rel_index()[:, None] == np.arange(2 * MAX_REL + 1)[None, :], F32)
    d_rel = jnp.einsum("hj,jr->hr", dcol, onehot, precision=lax.Precision.HIGHEST)

    big_out = {}
    for (nm, _, _, _, wt, mt, vt), ps, r2 in zip(first + rest, pairs_in + pairs_rest, recv2_in + recv2_rest):
        g_, d_, m_, v_ = _adamw_big("adamw_" + nm, ps, r2, wt[0], mt[0], vt[0], chip_idx)
        big_out[nm] = tuple(q[None] for q in (g_, d_, m_, v_))

    small = [("norm_in_g", d_norm_in_g, norm_in_g, m_norm_in_g, v_norm_in_g),
             ("rel_table", d_rel, rel_table, m_rel_table, v_rel_table),
             ("b_dw", d_b_dw, b_dw, m_b_dw, v_b_dw),
             ("conv_ln_g", d_ln_g, conv_ln_g, m_conv_ln_g, v_conv_ln_g),
             ("conv_ln_b", d_ln_b, conv_ln_b, m_conv_ln_b, v_conv_ln_b),
             ("b_pw", cs_dxin[1], b_pw, m_b_pw, v_b_pw),
             ("attn_out_g", d_gains[0], attn_out_g, m_attn_out_g, v_attn_out_g),
             ("conv_out_g", d_gains[1], conv_out_g, m_conv_out_g, v_conv_out_g),
             ("ple_norm_g", d_ple_norm_g, ple_norm_g, m_ple_norm_g, v_ple_norm_g),
             ("b_ple_gate", d_b_gate, b_ple_gate, m_b_ple_gate, v_b_ple_gate),
             ("final_g", d_final_g, final_g, m_final_g, v_final_g)]
    g_parts = [g for _, g, *_ in small] + [d_w_dw[:CONV_K], loss_v[0, :1]]
    n_small = sum(int(np.prod(q.shape)) for q in g_parts)
    rows = -(-n_small // (8 * LANES)) * 8
    g_all = _all_reduce_small(_pack(g_parts, rows))
    shapes = [wt.shape for _, _, wt, _, _ in small] + [(CONV_K, a), (1,)]
    g_un = _unpack(g_all, shapes)
    loss = g_un[-1][0]
    g_w_dw = lax.dynamic_slice_in_dim(g_un[-2], me * (a // N_DEV), a // N_DEV, axis=1)[None]
    g_small = g_un[:len(small)] + [g_w_dw]
    rows2 = -(-(n_small - 1) // (8 * LANES)) * 8
    pk = lambda parts: _pack(parts, rows2)
    d_pk, m_pk, v_pk = _adamw_small(pk(g_small), pk([wt for _, _, wt, _, _ in small] + [w_dw]),
                                    pk([mt for _, _, _, mt, _ in small] + [m_w_dw]), pk([vt for _, _, _, _, vt in small] + [v_w_dw]))
    shapes2 = [wt.shape for _, _, wt, _, _ in small] + [w_dw.shape]
    names = [nm for nm, *_ in small] + ["w_dw"]
    small_out = {nm: (g, dd, mm, vv) for nm, g, dd, mm, vv in
                 zip(names, g_small, _unpack(d_pk, shapes2), _unpack(m_pk, shapes2), _unpack(v_pk, shapes2))}

    order = ["norm_in_g", "w_in", "rel_table", "w_dw", "b_dw", "conv_ln_g", "conv_ln_b", "w_pw", "b_pw", "attn_out_g",
             "conv_out_g", "w_out", "ple_norm_g", "w_ple_gate", "b_ple_gate", "w_ple", "final_g"]
    alias = {"w_ple_gate": "w_gate"}
    res = {nm: (big_out[alias.get(nm, nm)] if alias.get(nm, nm) in big_out else small_out[nm]) for nm in order}
    outs = [loss, grad_x[None]]
    for kind in range(4):
        outs += [res[nm][kind].reshape(w_shape) for nm, w_shape in zip(order, [
            norm_in_g.shape, w_in.shape, rel_table.shape, w_dw.shape, b_dw.shape, conv_ln_g.shape, conv_ln_b.shape, w_pw.shape,
            b_pw.shape, attn_out_g.shape, conv_out_g.shape, w_out.shape, ple_norm_g.shape, w_ple_gate.shape, b_ple_gate.shape,
            w_ple.shape, final_g.shape])]
    return tuple(outs)
```

```python
import functools

import numpy as np
import jax
import jax.numpy as jnp
from jax import lax
from jax.experimental import pallas as pl
from jax.experimental.pallas import tpu as pltpu

F32 = jnp.float32
BF16 = jnp.bfloat16
MESH = pl.DeviceIdType.MESH

CHUNK = 64
LEFT_CHUNKS = 8
HEAD_DIM = 128
MAX_REL = 256
CONV_K = 31
EPS = 1e-6
NEG_INF = -1e30
ADAM_LR = 0.001
ADAM_B1 = 0.9
ADAM_B2 = 0.999
ADAM_EPS = 1e-08
ADAM_WD = 0.01
ADAM_STEP = 10

N_DEV = 8
QB = CHUNK * LEFT_CHUNKS
KB = 2 * QB
HALO = 32
CONV_COLS = 256
CONV_ROWS = 64
LANES = 128
VMEM_CAP = 60 * 1024 * 1024
NT = (((1,), (1,)), ((), ()))
TN = (((0,), (0,)), ((), ()))
NN = (((1,), (0,)), ((), ()))


def _cparams(sem, est_bytes):
    return pltpu.CompilerParams(dimension_semantics=sem, vmem_limit_bytes=int(min(VMEM_CAP, max(32 << 20, 2 * est_bytes))))


def _nbytes(shape, dtype):
    return int(np.prod(shape)) * jnp.dtype(dtype).itemsize


def _row_block(rows, bytes_per_row, budget=12 << 20):
    tb = rows
    while tb > 16 and (tb * bytes_per_row * 2 > budget or rows % tb):
        tb //= 2
    assert rows % tb == 0
    return tb


def _sigmoid(x):
    return 1.0 / (1.0 + jnp.exp(-x))


def _cast_bf16(w, name):
    r, c = w.shape
    tb = _row_block(r, c * 6)

    def body(w_ref, o_ref):
        o_ref[...] = w_ref[...].astype(BF16)

    return pl.pallas_call(
        body, name=name, grid=(r // tb,),
        in_specs=[pl.BlockSpec((tb, c), lambda i: (i, 0))], out_specs=pl.BlockSpec((tb, c), lambda i: (i, 0)),
        out_shape=jax.ShapeDtypeStruct((r, c), BF16), compiler_params=_cparams(("parallel",), tb * c * 12))(w)


def _rms_fwd(x, g, name):
    t, d = x.shape
    tb = _row_block(t, d * 6)

    def body(x_ref, g_ref, o_ref):
        xv = x_ref[...]
        r = lax.rsqrt(jnp.mean(xv * xv, axis=-1, keepdims=True) + EPS)
        o_ref[...] = (xv * r * g_ref[...]).astype(BF16)

    return pl.pallas_call(
        body, name=name, grid=(t // tb,),
        in_specs=[pl.BlockSpec((tb, d), lambda i: (i, 0)), pl.BlockSpec((1, d), lambda i: (0, 0))],
        out_specs=pl.BlockSpec((tb, d), lambda i: (i, 0)),
        out_shape=jax.ShapeDtypeStruct((t, d), BF16), compiler_params=_cparams(("parallel",), tb * d * 12))(x, g)


def _rms_bwd(dy, x, g, resid, name, with_bf16):
    t, d = x.shape
    tb = _row_block(t, d * 18)

    def body(dy_ref, x_ref, g_ref, res_ref, *outs):
        i = pl.program_id(0)
        xv = x_ref[...]
        r = lax.rsqrt(jnp.mean(xv * xv, axis=-1, keepdims=True) + EPS)
        n = xv * r
        dyv = dy_ref[...].astype(F32)
        dn = dyv * g_ref[...]
        dx = res_ref[...] + r * (dn - n * jnp.mean(dn * n, axis=-1, keepdims=True))
        outs[0][...] = dx
        if with_bf16:
            outs[1][...] = dx.astype(BF16)
        dg_ref = outs[-1]
        part = jnp.sum(dyv * n, axis=0, keepdims=True)

        @pl.when(i == 0)
        def _():
            dg_ref[...] = part

        @pl.when(i > 0)
        def _():
            dg_ref[...] += part

    row = pl.BlockSpec((tb, d), lambda i: (i, 0))
    vec = pl.BlockSpec((1, d), lambda i: (0, 0))
    out_shape = [jax.ShapeDtypeStruct((t, d), F32)] + ([jax.ShapeDtypeStruct((t, d), BF16)] if with_bf16 else []) + [
        jax.ShapeDtypeStruct((1, d), F32)]
    out_specs = [row] + ([row] if with_bf16 else []) + [vec]
    return pl.pallas_call(
        body, name=name, grid=(t // tb,), in_specs=[row, row, vec, row], out_specs=out_specs, out_shape=out_shape,
        compiler_params=_cparams(("arbitrary",), tb * d * 36))(dy, x, g, resid)


def _final(h2, tgt, gate, pe, g):
    t, d = h2.shape
    tb = _row_block(t, d * 20)

    def body(h_ref, t_ref, gate_ref, pe_ref, g_ref, dh_ref, dpe_ref, dgl_ref, loss_ref, dfg_ref, dbg_ref):
        i = pl.program_id(0)
        hv = h_ref[...]
        r = lax.rsqrt(jnp.mean(hv * hv, axis=-1, keepdims=True) + EPS)
        n = hv * r
        gv = g_ref[...]
        err = n * gv - t_ref[...]
        loss_part = 0.5 * jnp.sum(jnp.mean(err * err, axis=-1, keepdims=True), axis=0, keepdims=True)
        dy = err * (1.0 / d)
        dn = dy * gv
        dh = r * (dn - n * jnp.mean(dn * n, axis=-1, keepdims=True))
        dh_ref[...] = dh
        gt = gate_ref[...].astype(F32)
        dpe_ref[...] = (dh * gt).astype(BF16)
        dgl = dh * pe_ref[...].astype(F32) * gt * (1.0 - gt)
        dgl_ref[...] = dgl.astype(BF16)
        dfg = jnp.sum(dy * n, axis=0, keepdims=True)
        dbg = jnp.sum(dgl, axis=0, keepdims=True)

        @pl.when(i == 0)
        def _():
            loss_ref[...] = jnp.broadcast_to(loss_part, loss_ref.shape)
            dfg_ref[...] = dfg
            dbg_ref[...] = dbg

        @pl.when(i > 0)
        def _():
            loss_ref[...] += jnp.broadcast_to(loss_part, loss_ref.shape)
            dfg_ref[...] += dfg
            dbg_ref[...] += dbg

    row = pl.BlockSpec((tb, d), lambda i: (i, 0))
    vec = pl.BlockSpec((1, d), lambda i: (0, 0))
    return pl.pallas_call(
        body, name="final_loss_bwd", grid=(t // tb,), in_specs=[row, row, row, row, vec],
        out_specs=[row, row, row, pl.BlockSpec((1, LANES), lambda i: (0, 0)), vec, vec],
        out_shape=[jax.ShapeDtypeStruct((t, d), F32), jax.ShapeDtypeStruct((t, d), BF16), jax.ShapeDtypeStruct((t, d), BF16),
                   jax.ShapeDtypeStruct((1, LANES), F32), jax.ShapeDtypeStruct((1, d), F32), jax.ShapeDtypeStruct((1, d), F32)],
        compiler_params=_cparams(("arbitrary",), tb * d * 40))(h2, tgt, gate, pe, g)


def _post_fwd(o, yc, proj7, gains):
    t, a = o.shape
    tb = _row_block(t, a * 14)
    nb = t // tb

    def body(o_ref, yc_ref, z_ref, g_ref, out_ref):
        s = pl.program_id(0)
        xin = jnp.where(s == 0, o_ref[...], yc_ref[...])
        r = lax.rsqrt(jnp.mean(xin * xin, axis=-1, keepdims=True) + EPS)
        z = z_ref[...].astype(F32)
        out_ref[...] = (xin * r * g_ref[...] * (z * _sigmoid(z))).astype(BF16)

    return pl.pallas_call(
        body, name="post_fwd", grid=(2, nb),
        in_specs=[pl.BlockSpec((tb, a), lambda s, i: (i * (1 - s) + (nb - 1) * s, 0)),
                  pl.BlockSpec((tb, a), lambda s, i: (i * s, 0)),
                  pl.BlockSpec((None, tb, a), lambda s, i: (3 + 3 * s, i, 0)),
                  pl.BlockSpec((None, 1, a), lambda s, i: (s, 0, 0))],
        out_specs=pl.BlockSpec((tb, a), lambda s, i: (i, s)),
        out_shape=jax.ShapeDtypeStruct((t, 2 * a), BF16),
        compiler_params=_cparams(("arbitrary", "arbitrary"), tb * a * 28))(o, yc, proj7, gains)


def _post_bwd(dycat, o, yc, proj7, gains):
    t, a = o.shape
    tb = _row_block(t, a * 24)
    nb = t // tb

    def body(dy_ref, o_ref, yc_ref, z_ref, g_ref, dz_ref, dx_ref, dg_ref, cs_ref):
        s = pl.program_id(0)
        i = pl.program_id(1)
        xin = jnp.where(s == 0, o_ref[...], yc_ref[...])
        r = lax.rsqrt(jnp.mean(xin * xin, axis=-1, keepdims=True) + EPS)
        n = xin * r
        z = z_ref[...].astype(F32)
        sg = _sigmoid(z)
        gv = g_ref[...]
        dy = dy_ref[...]
        dz_ref[...] = (dy * (n * gv) * (sg * (1.0 + z * (1.0 - sg)))).astype(BF16)
        dyn = dy * (z * sg)
        dn = dyn * gv
        dx = r * (dn - n * jnp.mean(dn * n, axis=-1, keepdims=True))
        dxb = dx.astype(BF16)
        dx_ref[...] = dxb
        dg = jnp.sum(dyn * n, axis=0, keepdims=True)
        cs = jnp.sum(dxb.astype(F32), axis=0, keepdims=True)

        @pl.when(i == 0)
        def _():
            dg_ref[...] = dg
            cs_ref[...] = cs

        @pl.when(i > 0)
        def _():
            dg_ref[...] += dg
            cs_ref[...] += cs

    vec = pl.BlockSpec((None, 1, a), lambda s, i: (s, 0, 0))
    return pl.pallas_call(
        body, name="post_bwd", grid=(2, nb),
        in_specs=[pl.BlockSpec((tb, a), lambda s, i: (i, s)),
                  pl.BlockSpec((tb, a), lambda s, i: (i * (1 - s) + (nb - 1) * s, 0)),
                  pl.BlockSpec((tb, a), lambda s, i: (i * s, 0)),
                  pl.BlockSpec((None, tb, a), lambda s, i: (3 + 3 * s, i, 0)),
                  vec],
        out_specs=[pl.BlockSpec((None, tb, a), lambda s, i: (3 + 3 * s, i, 0)),
                   pl.BlockSpec((None, tb, a), lambda s, i: (s, i, 0)), vec, vec],
        out_shape=[jax.ShapeDtypeStruct((7, t, a), BF16), jax.ShapeDtypeStruct((2, t, a), BF16),
                   jax.ShapeDtypeStruct((2, 1, a), F32), jax.ShapeDtypeStruct((2, 1, a), F32)],
        compiler_params=_cparams(("arbitrary", "arbitrary"), tb * a * 48))(dycat, o, yc, proj7, gains)


def _ln_silu_fwd(u1, g, b):
    t, c = u1.shape
    tb = _row_block(t, c * 6)

    def body(u_ref, g_ref, b_ref, o_ref):
        u = u_ref[...]
        mu = jnp.mean(u, axis=-1, keepdims=True)
        xc = u - mu
        rstd = lax.rsqrt(jnp.mean(xc * xc, axis=-1, keepdims=True) + EPS)
        u2 = xc * rstd * g_ref[...] + b_ref[...]
        o_ref[...] = (u2 * _sigmoid(u2)).astype(BF16)

    row = pl.BlockSpec((tb, c), lambda i: (i, 0))
    vec = pl.BlockSpec((1, c), lambda i: (0, 0))
    return pl.pallas_call(
        body, name="ln_silu_fwd", grid=(t // tb,), in_specs=[row, vec, vec], out_specs=row,
        out_shape=jax.ShapeDtypeStruct((t, c), BF16), compiler_params=_cparams(("parallel",), tb * c * 12))(u1, g, b)


def _ln_silu_bwd(du3, u1, g, b):
    t, c = u1.shape
    tb = _row_block(t, c * 12)

    def body(d_ref, u_ref, g_ref, b_ref, du_ref, dg_ref, db_ref, cs_ref):
        i = pl.program_id(0)
        u = u_ref[...]
        mu = jnp.mean(u, axis=-1, keepdims=True)
        xc = u - mu
        rstd = lax.rsqrt(jnp.mean(xc * xc, axis=-1, keepdims=True) + EPS)
        xh = xc * rstd
        gv = g_ref[...]
        u2 = xh * gv + b_ref[...]
        sg = _sigmoid(u2)
        du2 = d_ref[...] * (sg * (1.0 + u2 * (1.0 - sg)))
        dxh = du2 * gv
        du1 = rstd * (dxh - jnp.mean(dxh, axis=-1, keepdims=True) - xh * jnp.mean(dxh * xh, axis=-1, keepdims=True))
        du_ref[...] = du1
        dg = jnp.sum(du2 * xh, axis=0, keepdims=True)
        db = jnp.sum(du2, axis=0, keepdims=True)
        cs = jnp.sum(du1, axis=0, keepdims=True)

        @pl.when(i == 0)
        def _():
            dg_ref[...] = dg
            db_ref[...] = db
            cs_ref[...] = cs

        @pl.when(i > 0)
        def _():
            dg_ref[...] += dg
            db_ref[...] += db
            cs_ref[...] += cs

    row = pl.BlockSpec((tb, c), lambda i: (i, 0))
    vec = pl.BlockSpec((1, c), lambda i: (0, 0))
    vs = jax.ShapeDtypeStruct((1, c), F32)
    return pl.pallas_call(
        body, name="ln_silu_bwd", grid=(t // tb,), in_specs=[row, row, vec, vec], out_specs=[row, vec, vec, vec],
        out_shape=[jax.ShapeDtypeStruct((t, c), F32), vs, vs, vs],
        compiler_params=_cparams(("arbitrary",), tb * c * 24))(du3, u1, g, b)


def _conv_block(t):
    tb = 512
    while t % tb:
        tb //= 2
    assert tb >= HALO
    return tb


def _conv_fwd(proj7, w_dw, b_dw):
    _, t, c = proj7.shape
    tb = _conv_block(t)
    hb = tb // HALO
    cw = CONV_COLS

    def body(a_ref, g_ref, ah_ref, gh_ref, w_ref, b_ref, u1_ref, ext):
        i = pl.program_id(1)
        halo = ah_ref[...].astype(F32) * _sigmoid(gh_ref[...].astype(F32))
        ext[0:HALO, :] = jnp.where(i > 0, halo, 0.0)
        ext[HALO:HALO + tb, :] = a_ref[...].astype(F32) * _sigmoid(g_ref[...].astype(F32))
        bias = b_ref[...]
        for r0 in range(0, tb, CONV_ROWS):
            acc = jnp.broadcast_to(bias, (CONV_ROWS, cw))
            for k in range(CONV_K):
                off = r0 + HALO - (CONV_K - 1) + k
                acc = acc + ext[off:off + CONV_ROWS, :] * w_ref[k:k + 1, :]
            u1_ref[r0:r0 + CONV_ROWS, :] = acc

    blk = lambda seg: pl.BlockSpec((None, tb, cw), lambda j, i: (seg, i, j))
    halo = lambda seg: pl.BlockSpec((None, HALO, cw), lambda j, i: (seg, jnp.maximum(i * hb - 1, 0), j))
    return pl.pallas_call(
        body, name="conv_fwd", grid=(c // cw, t // tb),
        in_specs=[blk(4), blk(5), halo(4), halo(5), pl.BlockSpec((32, cw), lambda j, i: (0, j)),
                  pl.BlockSpec((1, cw), lambda j, i: (0, j))],
        out_specs=pl.BlockSpec((tb, cw), lambda j, i: (i, j)),
        out_shape=jax.ShapeDtypeStruct((t, c), F32),
        scratch_shapes=[pltpu.VMEM((tb + HALO, cw), F32)],
        compiler_params=_cparams(("parallel", "arbitrary"), tb * cw * 16))(proj7, proj7, proj7, proj7, w_dw, b_dw)


def _conv_bwd(dproj7, du1, proj7, w_dw):
    _, t, c = proj7.shape
    tb = _conv_block(t)
    hb = tb // HALO
    nb = t // tb
    cw = CONV_COLS

    def body(dp_ref, du_ref, duh_ref, a_ref, g_ref, ah_ref, gh_ref, w_ref, out_ref, dw_ref, dext, uext, dwp):
        del dp_ref
        i = pl.program_id(1)
        dext[0:tb, :] = du_ref[...]
        dext[tb:tb + HALO, :] = jnp.where(i < nb - 1, duh_ref[...], 0.0)
        halo = ah_ref[...].astype(F32) * _sigmoid(gh_ref[...].astype(F32))
        uext[0:HALO, :] = jnp.where(i > 0, halo, 0.0)
        for r0 in range(0, tb, CONV_ROWS):
            a_c = a_ref[r0:r0 + CONV_ROWS, :].astype(F32)
            s_c = _sigmoid(g_ref[r0:r0 + CONV_ROWS, :].astype(F32))
            uext[HALO + r0:HALO + r0 + CONV_ROWS, :] = a_c * s_c
            acc = jnp.zeros((CONV_ROWS, cw), F32)
            for k in range(CONV_K):
                off = r0 + (CONV_K - 1) - k
                acc = acc + dext[off:off + CONV_ROWS, :] * w_ref[k:k + 1, :]
            out_ref[0, r0:r0 + CONV_ROWS, :] = (acc * s_c).astype(BF16)
            out_ref[1, r0:r0 + CONV_ROWS, :] = (acc * a_c * s_c * (1.0 - s_c)).astype(BF16)
        for k in range(CONV_K):
            acc = jnp.zeros((8, cw), F32)
            for r0 in range(0, tb, CONV_ROWS):
                off = r0 + HALO - (CONV_K - 1) + k
                prod = dext[r0:r0 + CONV_ROWS, :] * uext[off:off + CONV_ROWS, :]
                acc = acc + jnp.sum(prod.reshape(CONV_ROWS // 8, 8, cw), axis=0)
            dwp[k:k + 1, :] = jnp.sum(acc, axis=0, keepdims=True)
        dwp[CONV_K:32, :] = jnp.zeros((32 - CONV_K, cw), F32)

        @pl.when(i == 0)
        def _():
            dw_ref[...] = dwp[...]

        @pl.when(i > 0)
        def _():
            dw_ref[...] += dwp[...]

    blk = lambda seg: pl.BlockSpec((None, tb, cw), lambda j, i: (seg, i, j))
    halo = lambda seg: pl.BlockSpec((None, HALO, cw), lambda j, i: (seg, jnp.maximum(i * hb - 1, 0), j))
    est = tb * cw * 40
    return pl.pallas_call(
        body, name="conv_bwd", grid=(c // cw, nb),
        in_specs=[pl.BlockSpec(memory_space=pl.ANY),
                  pl.BlockSpec((tb, cw), lambda j, i: (i, j)),
                  pl.BlockSpec((HALO, cw), lambda j, i: (jnp.minimum((i + 1) * hb, nb * hb - 1), j)),
                  blk(4), blk(5), halo(4), halo(5), pl.BlockSpec((32, cw), lambda j, i: (0, j))],
        out_specs=[pl.BlockSpec((2, tb, cw), lambda j, i: (2, i, j)), pl.BlockSpec((32, cw), lambda j, i: (0, j))],
        out_shape=[jax.ShapeDtypeStruct(dproj7.shape, BF16), jax.ShapeDtypeStruct((32, c), F32)],
        scratch_shapes=[pltpu.VMEM((tb + HALO, cw), F32), pltpu.VMEM((tb + HALO, cw), F32), pltpu.VMEM((32, cw), F32)],
        input_output_aliases={0: 0},
        compiler_params=_cparams(("parallel", "arbitrary"), est))(dproj7, du1, du1, proj7, proj7, proj7, proj7, w_dw)


class _Comm:
    def __init__(self, inputs, out_shape, sems, start, finish, aliased=False):
        self.inputs, self.out_shape, self.sems, self.start, self.finish, self.aliased = inputs, out_shape, sems, start, finish, aliased


def _pcall(body, *, name, grid, in_specs, out_specs, out_shape, operands, scratch=(), sem, est, comm=None, aliases=None):
    aliases = dict(aliases or {})
    if comm is None:
        return pl.pallas_call(body, name=name, grid=grid, in_specs=in_specs, out_specs=out_specs, out_shape=out_shape,
                              scratch_shapes=list(scratch), input_output_aliases=aliases,
                              compiler_params=_cparams(sem, est))(*operands)
    n_in, n_out, n_scr = len(in_specs), len(out_specs), len(scratch)
    n_ci, n_co, n_sem = len(comm.inputs), len(comm.out_shape), len(comm.sems)
    last_ids = [g - 1 for g in grid]

    def wrapped(*refs):
        ins, cin = refs[:n_in], refs[n_in:n_in + n_ci]
        outs = refs[n_in + n_ci:n_in + n_ci + n_out]
        cout = refs[n_in + n_ci + n_out:n_in + n_ci + n_out + n_co]
        scr = refs[n_in + n_ci + n_out + n_co:n_in + n_ci + n_out + n_co + n_scr]
        sems = refs[len(refs) - n_sem:]
        ids = [pl.program_id(k) for k in range(len(grid))]
        first = functools.reduce(jnp.logical_and, [i == 0 for i in ids])
        last = functools.reduce(jnp.logical_and, [i == l for i, l in zip(ids, last_ids)])

        @pl.when(first)
        def _():
            comm.start(cin, cout, sems)

        body(*ins, *outs, *scr)

        @pl.when(last)
        def _():
            comm.finish(cin, cout, sems)

    hbm = pl.BlockSpec(memory_space=pl.ANY)
    if comm.aliased:
        aliases.update({n_in + k: n_out + k for k in range(n_ci)})
    res = pl.pallas_call(
        wrapped, name=name, grid=grid, in_specs=[*in_specs, *([hbm] * n_ci)], out_specs=[*out_specs, *([hbm] * n_co)],
        out_shape=[*out_shape, *comm.out_shape], scratch_shapes=[*scratch, *comm.sems], input_output_aliases=aliases,
        compiler_params=_cparams(("arbitrary",) * len(grid), est))(*operands, *comm.inputs)
    return list(res[:n_out]), list(res[n_out:])


def _rel_index():
    j = np.arange(KB)
    key_minus_query = np.where(j < KB - CHUNK, j, j - KB) - QB
    return (np.clip(-key_minus_query, -MAX_REL, MAX_REL) + MAX_REL).astype(np.int32)


def _row_bits(shape):
    return lax.broadcasted_iota(jnp.int32, shape, 0)


def _bias_build(tvec):
    h = tvec.shape[0]

    def body(t_ref, o_ref):
        x = jnp.broadcast_to(t_ref[...], (QB, KB))
        row = _row_bits((QB, KB))
        for bit in range(QB.bit_length() - 1):
            x = jnp.where(((row >> bit) & 1) == 1, pltpu.roll(x, 1 << bit, 1), x)
        qc = row // CHUNK
        kc = lax.broadcasted_iota(jnp.int32, (QB, KB), 1) // CHUNK
        o_ref[...] = jnp.where((kc >= qc) & (kc <= qc + LEFT_CHUNKS), x, NEG_INF)

    return pl.pallas_call(
        body, name="bias_build", grid=(h,), in_specs=[pl.BlockSpec((None, 1, KB), lambda i: (i, 0, 0))],
        out_specs=pl.BlockSpec((None, QB, KB), lambda i: (i, 0, 0)), out_shape=jax.ShapeDtypeStruct((h, QB, KB), F32),
        compiler_params=_cparams(("parallel",), QB * KB * 16))(tvec)


def _bias_grad(ds):
    h = ds.shape[0]

    def body(d_ref, o_ref):
        x = d_ref[...]
        row = _row_bits((QB, KB))
        for bit in range(QB.bit_length() - 1):
            x = jnp.where(((row >> bit) & 1) == 1, pltpu.roll(x, KB - (1 << bit), 1), x)
        o_ref[...] = jnp.sum(x, axis=0, keepdims=True)

    return pl.pallas_call(
        body, name="bias_grad", grid=(h,), in_specs=[pl.BlockSpec((None, QB, KB), lambda i: (i, 0, 0))],
        out_specs=pl.BlockSpec((None, 1, KB), lambda i: (i, 0, 0)), out_shape=jax.ShapeDtypeStruct((h, 1, KB), F32),
        compiler_params=_cparams(("parallel",), QB * KB * 16))(ds)


def _attn_specs(a, nb, clamp):
    hpa = a // HEAD_DIM
    cur = (lambda i: jnp.minimum(i, nb - 1)) if clamp else (lambda i: i)
    prev = lambda i: jnp.maximum(cur(i) - 1, 0)
    q = pl.BlockSpec((None, QB, HEAD_DIM), lambda h, i: (0, cur(i), h))
    kp = pl.BlockSpec((None, QB, HEAD_DIM), lambda h, i: (1, prev(i), h))
    kc = pl.BlockSpec((None, QB, HEAD_DIM), lambda h, i: (1, cur(i), h))
    vp = pl.BlockSpec((None, QB, HEAD_DIM), lambda h, i: (2, prev(i), h))
    vc = pl.BlockSpec((None, QB, HEAD_DIM), lambda h, i: (2, cur(i), h))
    bias = pl.BlockSpec((None, QB, KB), lambda h, i: (h, 0, 0))
    del hpa
    return q, kp, kc, vp, vc, bias


def _scores(q, kp, kc, b_ref, first):
    scale = HEAD_DIM ** -0.5
    pen = jnp.where(first, NEG_INF, 0.0).astype(F32)
    sp = lax.dot_general(q, kp, NT, preferred_element_type=F32) * scale + b_ref[:, 0:QB] + pen
    sc = lax.dot_general(q, kc, NT, preferred_element_type=F32) * scale + b_ref[:, QB:KB]
    m = jnp.maximum(jnp.max(sp, axis=-1, keepdims=True), jnp.max(sc, axis=-1, keepdims=True))
    ep = jnp.exp(sp - m)
    ec = jnp.exp(sc - m)
    l = jnp.sum(ep, axis=-1, keepdims=True) + jnp.sum(ec, axis=-1, keepdims=True)
    return ep, ec, l


def _attn_fwd(proj7, bias, comm=None):
    _, t, a = proj7.shape
    nb = t // QB
    h = a // HEAD_DIM

    def body(q_ref, kp_ref, kc_ref, vp_ref, vc_ref, b_ref, o_ref):
        i = pl.program_id(1)
        ep, ec, l = _scores(q_ref[...], kp_ref[...], kc_ref[...], b_ref, i == 0)
        o = (jnp.dot(ep.astype(BF16), vp_ref[...], preferred_element_type=F32)
             + jnp.dot(ec.astype(BF16), vc_ref[...], preferred_element_type=F32))
        o_ref[...] = o / l

    res = _pcall(
        body, name="attn_fwd", grid=(h, nb), in_specs=list(_attn_specs(a, nb, False)),
        out_specs=[pl.BlockSpec((QB, HEAD_DIM), lambda hh, i: (i, hh))],
        out_shape=[jax.ShapeDtypeStruct((t, a), F32)], operands=(proj7, proj7, proj7, proj7, proj7, bias),
        sem=("parallel", "arbitrary"), est=QB * KB * 40, comm=comm)
    return res[0] if comm is None else (res[0][0], res[1])


def _attn_bwd(dproj7, proj7, do, bias):
    _, t, a = proj7.shape
    nb = t // QB
    h = a // HEAD_DIM
    scale = HEAD_DIM ** -0.5

    def body(dp_ref, q_ref, kp_ref, kc_ref, vp_ref, vc_ref, b_ref, do_ref, out_ref, ds_ref, dq_c, dk_c, dv_c):
        del dp_ref
        i = pl.program_id(1)

        @pl.when(i < nb)
        def _():
            q = q_ref[...]
            kp = kp_ref[...]
            kc = kc_ref[...]
            do = do_ref[...]
            ep, ec, l = _scores(q, kp, kc, b_ref, i == 0)
            inv = 1.0 / l
            pp = ep * inv
            pc = ec * inv
            dpp = lax.dot_general(do, vp_ref[...], NT, preferred_element_type=F32)
            dpc = lax.dot_general(do, vc_ref[...], NT, preferred_element_type=F32)
            delta = jnp.sum(pp * dpp, axis=-1, keepdims=True) + jnp.sum(pc * dpc, axis=-1, keepdims=True)
            dsp = pp * (dpp - delta)
            dsc = pc * (dpc - delta)

            @pl.when(i == 0)
            def _():
                ds_ref[:, 0:QB] = dsp
                ds_ref[:, QB:KB] = dsc

            @pl.when(i > 0)
            def _():
                ds_ref[:, 0:QB] += dsp
                ds_ref[:, QB:KB] += dsc

            dspb = (dsp * scale).astype(BF16)
            dscb = (dsc * scale).astype(BF16)
            dq = jnp.dot(dspb, kp, preferred_element_type=F32) + jnp.dot(dscb, kc, preferred_element_type=F32)
            dkp = lax.dot_general(dspb, q, TN, preferred_element_type=F32)
            dkc = lax.dot_general(dscb, q, TN, preferred_element_type=F32)
            dvp = lax.dot_general(pp.astype(BF16), do, TN, preferred_element_type=F32)
            dvc = lax.dot_general(pc.astype(BF16), do, TN, preferred_element_type=F32)

            @pl.when(i > 0)
            def _():
                out_ref[0] = dq_c[...].astype(BF16)
                out_ref[1] = (dk_c[...] + dkp).astype(BF16)
                out_ref[2] = (dv_c[...] + dvp).astype(BF16)

            dq_c[...] = dq
            dk_c[...] = dkc
            dv_c[...] = dvc

        @pl.when(i == nb)
        def _():
            out_ref[0] = dq_c[...].astype(BF16)
            out_ref[1] = dk_c[...].astype(BF16)
            out_ref[2] = dv_c[...].astype(BF16)

    q, kp, kc, vp, vc, bspec = _attn_specs(a, nb, True)
    return pl.pallas_call(
        body, name="attn_bwd", grid=(h, nb + 1),
        in_specs=[pl.BlockSpec(memory_space=pl.ANY), q, kp, kc, vp, vc, bspec,
                  pl.BlockSpec((None, QB, HEAD_DIM), lambda hh, i: (0, jnp.minimum(i, nb - 1), hh))],
        out_specs=[pl.BlockSpec((3, QB, HEAD_DIM), lambda hh, i: (0, jnp.maximum(i - 1, 0), hh)),
                   pl.BlockSpec((None, QB, KB), lambda hh, i: (hh, 0, 0))],
        out_shape=[jax.ShapeDtypeStruct(dproj7.shape, BF16), jax.ShapeDtypeStruct((h, QB, KB), F32)],
        scratch_shapes=[pltpu.VMEM((QB, HEAD_DIM), F32)] * 3,
        input_output_aliases={0: 0},
        compiler_params=_cparams(("parallel", "arbitrary"), QB * KB * 60))(dproj7, proj7, proj7, proj7, proj7, proj7, bias, do)


def _tile(n, pref):
    t = min(n, pref)
    while n % t:
        t //= 2
    return t


def _mm(name, a, b, dn, grid, a_spec, b_spec, acc_shape, out_shape, out_specs, epilogue, extra=(), extra_specs=(), est=0,
        comm=None):
    nk = grid[2]
    n_extra = len(extra)
    single = not isinstance(out_shape, (list, tuple))
    if single:
        out_shape, out_specs = [out_shape], [out_specs]

    def body(*refs):
        a_ref, b_ref = refs[0], refs[1]
        ex = refs[2:2 + n_extra]
        outs = refs[2 + n_extra:-1]
        acc = refs[-1]
        k = pl.program_id(2)
        prod = lax.dot_general(a_ref[...].astype(BF16), b_ref[...].astype(BF16), dn, preferred_element_type=F32)

        @pl.when(k == 0)
        def _():
            acc[...] = prod

        @pl.when(k > 0)
        def _():
            acc[...] += prod

        @pl.when(k == nk - 1)
        def _():
            epilogue(acc[...], ex, outs)

    res = _pcall(body, name=name, grid=grid, in_specs=[a_spec, b_spec, *extra_specs], out_specs=list(out_specs),
                 out_shape=list(out_shape), operands=(a, b, *extra), scratch=[pltpu.VMEM(acc_shape, F32)],
                 sem=("parallel", "parallel", "arbitrary"), est=est, comm=comm)
    if comm is None:
        return res[0] if single else res
    return (res[0][0] if single else res[0]), res[1]


def _store(dtype):
    def ep(acc, ex, outs):
        outs[0][...] = acc.astype(dtype)
    return ep


def _mm_est(tm, tn, tk, out_bytes):
    return 2 * (tm * tk * 4 + tk * tn * 2) + tm * tn * 4 + 2 * tm * tn * out_bytes


def _mm_nn(name, a, b, out_dtype, bias=None, resid=None):
    m, k = a.shape
    n = b.shape[1]
    tm, tn, tk = _tile(m, 1024), _tile(n, 1024), _tile(k, 2048)
    extra, especs = [], []
    if bias is not None:
        extra.append(bias)
        especs.append(pl.BlockSpec((1, tn), lambda i, j, kk: (0, j)))
    if resid is not None:
        extra.append(resid)
        especs.append(pl.BlockSpec((tm, tn), lambda i, j, kk: (i, j)))

    def ep(acc, ex, outs):
        for r in ex:
            acc = acc + r[...]
        outs[0][...] = acc.astype(out_dtype)

    return _mm(name, a, b, NN, (m // tm, n // tn, k // tk),
               pl.BlockSpec((tm, tk), lambda i, j, kk: (i, kk)), pl.BlockSpec((tk, tn), lambda i, j, kk: (kk, j)),
               (tm, tn), jax.ShapeDtypeStruct((m, n), out_dtype), pl.BlockSpec((tm, tn), lambda i, j, kk: (i, j)),
               ep, extra, especs, _mm_est(tm, tn, tk, 12))


def _mm_nt(name, a, b, out_dtype):
    m, k = a.shape
    n = b.shape[0]
    tm, tn, tk = _tile(m, 1024), _tile(n, 1024), _tile(k, 2048)
    return _mm(name, a, b, NT, (m // tm, n // tn, k // tk),
               pl.BlockSpec((tm, tk), lambda i, j, kk: (i, kk)), pl.BlockSpec((tn, tk), lambda i, j, kk: (j, kk)),
               (tm, tn), jax.ShapeDtypeStruct((m, n), out_dtype), pl.BlockSpec((tm, tn), lambda i, j, kk: (i, j)),
               _store(out_dtype), est=_mm_est(tm, tn, tk, 4))


def _mm_tn(name, a, b, out_dtype):
    k, m = a.shape
    n = b.shape[1]
    tm, tn, tk = _tile(m, 1024), _tile(n, 1024), _tile(k, 2048)
    return _mm(name, a, b, TN, (m // tm, n // tn, k // tk),
               pl.BlockSpec((tk, tm), lambda i, j, kk: (kk, i)), pl.BlockSpec((tk, tn), lambda i, j, kk: (kk, j)),
               (tm, tn), jax.ShapeDtypeStruct((m, n), out_dtype), pl.BlockSpec((tm, tn), lambda i, j, kk: (i, j)),
               _store(out_dtype), est=_mm_est(tm, tn, tk, 4))


def _proj_fwd(xn, w_in, seg, comm=None):
    t, d = xn.shape
    ncol = w_in.shape[1]
    tm, tn, tk = _tile(t, 1024), _tile(seg, 2048), _tile(d, 2048)
    per = seg // tn
    return _mm("proj_fwd", xn, w_in, NN, (t // tm, ncol // tn, d // tk),
               pl.BlockSpec((tm, tk), lambda i, j, kk: (i, kk)), pl.BlockSpec((tk, tn), lambda i, j, kk: (kk, j)),
               (tm, tn), jax.ShapeDtypeStruct((ncol // seg, t, seg), BF16),
               pl.BlockSpec((None, tm, tn), lambda i, j, kk: (j // per, i, j % per)),
               _store(BF16), est=_mm_est(tm, tn, tk, 2), comm=comm)


def _proj_bwd_x(dproj7, w_in, comm=None):
    nseg, t, seg = dproj7.shape
    d = w_in.shape[0]
    tm, tn, tk = _tile(t, 1024), _tile(d, 1024), _tile(seg, 2048)
    per = seg // tk
    return _mm("proj_bwd_x", dproj7, w_in, NT, (t // tm, d // tn, nseg * per),
               pl.BlockSpec((None, tm, tk), lambda i, j, kk: (kk // per, i, kk % per)),
               pl.BlockSpec((tn, tk), lambda i, j, kk: (j, kk)),
               (tm, tn), jax.ShapeDtypeStruct((t, d), F32), pl.BlockSpec((tm, tn), lambda i, j, kk: (i, j)),
               _store(F32), est=_mm_est(tm, tn, tk, 4), comm=comm)


def _proj_bwd_w(xn, dproj7, comm=None):
    nseg, t, seg = dproj7.shape
    d = xn.shape[1]
    tm, tn, tk = _tile(d, 1024), _tile(seg, 2048), _tile(t, 2048)
    per = seg // tn
    return _mm("proj_bwd_w", xn, dproj7, TN, (d // tm, nseg * per, t // tk),
               pl.BlockSpec((tk, tm), lambda i, j, kk: (kk, i)),
               pl.BlockSpec((None, tk, tn), lambda i, j, kk: (j // per, kk, j % per)),
               (tm, tn), jax.ShapeDtypeStruct((d, nseg * seg), BF16), pl.BlockSpec((tm, tn), lambda i, j, kk: (i, j)),
               _store(BF16), est=_mm_est(tm, tn, tk, 2), comm=comm)


def _gate_fwd(hn, w_g, b_g, h1, p, w_ple):
    t, d = hn.shape
    pd = p.shape[1]
    tm, tn, tk = _tile(t, 1024), _tile(d, 1024), _tile(d, 1024)

    def ep(acc, ex, outs):
        b_ref, h1_ref, p_ref, wp_ref = ex
        gate = _sigmoid(acc + b_ref[...])
        pe = jnp.dot(p_ref[...].astype(BF16), wp_ref[...], preferred_element_type=F32)
        outs[0][...] = h1_ref[...] + gate * pe
        outs[1][...] = gate.astype(BF16)
        outs[2][...] = pe.astype(BF16)

    tile = pl.BlockSpec((tm, tn), lambda i, j, kk: (i, j))
    return _mm("gate_fwd", hn, w_g, NN, (t // tm, d // tn, d // tk),
               pl.BlockSpec((tm, tk), lambda i, j, kk: (i, kk)), pl.BlockSpec((tk, tn), lambda i, j, kk: (kk, j)),
               (tm, tn),
               [jax.ShapeDtypeStruct((t, d), F32), jax.ShapeDtypeStruct((t, d), BF16), jax.ShapeDtypeStruct((t, d), BF16)],
               [tile, tile, tile], ep, [b_g, h1, p, w_ple],
               [pl.BlockSpec((1, tn), lambda i, j, kk: (0, j)), tile, pl.BlockSpec((tm, pd), lambda i, j, kk: (i, 0)),
                pl.BlockSpec((pd, tn), lambda i, j, kk: (0, j))],
               _mm_est(tm, tn, tk, 16) + 2 * (tm * pd * 4 + pd * tn * 2))


def _coords():
    return lax.axis_index("x"), lax.axis_index("y"), lax.axis_index("c")


def _row_slice(rows):
    return lambda ref, o: ref.at[pl.ds(pl.multiple_of(o * rows, rows), rows), :]


def _col_slice(cols):
    return lambda ref, o: ref.at[:, pl.ds(pl.multiple_of(o * cols, cols), cols)]


def _all_gather(shards, slicers, full_shapes):
    n = len(shards)

    def body(*refs):
        srcs, outs = refs[:n], refs[n:2 * n]
        send_sems, recv_sems, local_sems = refs[2 * n:]
        x, y, c = _coords()
        me, sibling = (x, y, c), (x, y, 1 - c)
        chips = [(1 - x, y), (x, 1 - y), (1 - x, 1 - y)]

        def copy(w, k, block, to, src=None):
            dst = slicers[w](outs[w], 4 * block[0] + 2 * block[1] + block[2])
            return pltpu.make_async_remote_copy(
                src_ref=dst if src is None else src, dst_ref=dst, send_sem=send_sems.at[w, k], recv_sem=recv_sems.at[w, k],
                device_id=to, device_id_type=MESH)

        mine = [pltpu.make_async_copy(srcs[w], slicers[w](outs[w], 4 * x + 2 * y + c), local_sems.at[w]) for w in range(n)]
        for cp in mine:
            cp.start()
        first = []
        for w in range(n):
            first.append(copy(w, 0, me, sibling, src=srcs[w]))
            first += [copy(w, 1 + j, me, (*chip, c), src=srcs[w]) for j, chip in enumerate(chips)]
        for cp in first:
            cp.start()
        passed = []
        for j, chip in enumerate(chips):
            for w in range(n):
                copy(w, 1 + j, (*chip, c), me).wait_recv()
                fwd = copy(w, 4 + j, (*chip, c), sibling)
                fwd.start()
                passed.append(fwd)
        for w in range(n):
            copy(w, 0, sibling, me).wait_recv()
            for j, chip in enumerate(chips):
                copy(w, 4 + j, (*chip, 1 - c), me).wait_recv()
        for cp in first + passed:
            cp.wait_send()
        for cp in mine:
            cp.wait()

    hbm = pl.BlockSpec(memory_space=pl.ANY)
    return pl.pallas_call(
        body, name="all_gather_weights",
        out_shape=[jax.ShapeDtypeStruct(s, a.dtype) for s, a in zip(full_shapes, shards)],
        in_specs=[hbm] * n, out_specs=[hbm] * n,
        scratch_shapes=[pltpu.SemaphoreType.DMA((n, 7)), pltpu.SemaphoreType.DMA((n, 7)), pltpu.SemaphoreType.DMA((n,))],
        compiler_params=pltpu.CompilerParams(has_side_effects=True))(*shards)


def _gather_ici_comm(shards, slicers, full_shapes):
    n = len(shards)

    def descs(cin, cout, sems):
        send_sems, recv_sems, local_sems = sems
        x, y, c = _coords()
        me, sibling = (x, y, c), (x, y, 1 - c)
        chips = [(1 - x, y), (x, 1 - y), (1 - x, 1 - y)]

        def copy(w, k, block, to, src=None):
            dst = slicers[w](cout[w], 4 * block[0] + 2 * block[1] + block[2])
            return pltpu.make_async_remote_copy(
                src_ref=dst if src is None else src, dst_ref=dst, send_sem=send_sems.at[w, k], recv_sem=recv_sems.at[w, k],
                device_id=to, device_id_type=MESH)

        mine = [pltpu.make_async_copy(cin[w], slicers[w](cout[w], 4 * x + 2 * y + c), local_sems.at[w]) for w in range(n)]
        sends, recvs = [], []
        for w in range(n):
            sends.append(copy(w, 0, me, sibling, src=cin[w]))
            recvs.append(copy(w, 0, sibling, me))
            for j, chip in enumerate(chips):
                sends.append(copy(w, 1 + j, me, (*chip, c), src=cin[w]))
                recvs.append(copy(w, 1 + j, (*chip, c), me))
        return mine, sends, recvs

    def start(cin, cout, sems):
        mine, sends, _ = descs(cin, cout, sems)
        for cp in mine + sends:
            cp.start()

    def finish(cin, cout, sems):
        mine, sends, recvs = descs(cin, cout, sems)
        for cp in recvs:
            cp.wait_recv()
        for cp in sends:
            cp.wait_send()
        for cp in mine:
            cp.wait()

    return _Comm(list(shards), [jax.ShapeDtypeStruct(s, a.dtype) for s, a in zip(full_shapes, shards)],
                 [pltpu.SemaphoreType.DMA((n, 4)), pltpu.SemaphoreType.DMA((n, 4)), pltpu.SemaphoreType.DMA((n,))], start, finish)


def _gather_d2d_comm(fulls, slicers):
    n = len(fulls)

    def descs(cout, sems):
        send_sems, recv_sems = sems
        x, y, c = _coords()
        sibling = (x, y, 1 - c)
        chips = [(1 - x, y), (x, 1 - y), (1 - x, 1 - y)]
        sends, recvs = [], []
        for w in range(n):
            for j, (px, py) in enumerate(chips):
                for core, out in ((c, sends), (1 - c, recvs)):
                    blk = slicers[w](cout[w], 4 * px + 2 * py + core)
                    out.append(pltpu.make_async_remote_copy(
                        src_ref=blk, dst_ref=blk, send_sem=send_sems.at[w, j], recv_sem=recv_sems.at[w, j],
                        device_id=sibling, device_id_type=MESH))
        return sends, recvs

    def start(cin, cout, sems):
        for cp in descs(cout, sems)[0]:
            cp.start()

    def finish(cin, cout, sems):
        sends, recvs = descs(cout, sems)
        for cp in recvs:
            cp.wait_recv()
        for cp in sends:
            cp.wait_send()

    return _Comm(list(fulls), [jax.ShapeDtypeStruct(f.shape, f.dtype) for f in fulls],
                 [pltpu.SemaphoreType.DMA((n, 3)), pltpu.SemaphoreType.DMA((n, 3))], start, finish, aliased=True)


def _chip_exchange_comm(pairsums):
    n = len(pairsums)

    def descs(cin, cout, sems):
        send_sems, recv_sems = sems
        x, y, c = _coords()
        copies = []
        for w in range(n):
            for r in (1, 2, 3):
                px, py = x ^ (r >> 1), y ^ (r & 1)
                copies.append(pltpu.make_async_remote_copy(
                    src_ref=cin[w].at[2 * px + py], dst_ref=cout[w].at[r - 1],
                    send_sem=send_sems.at[w, r - 1], recv_sem=recv_sems.at[w, r - 1], device_id=(px, py, c), device_id_type=MESH))
        return copies

    def start(cin, cout, sems):
        for cp in descs(cin, cout, sems):
            cp.start()

    def finish(cin, cout, sems):
        copies = descs(cin, cout, sems)
        for cp in copies:
            cp.wait_recv()
        for cp in copies:
            cp.wait_send()

    return _Comm(list(pairsums), [jax.ShapeDtypeStruct((3, *p.shape[1:]), p.dtype) for p in pairsums],
                 [pltpu.SemaphoreType.DMA((n, 3)), pltpu.SemaphoreType.DMA((n, 3))], start, finish)


def _pair_exchange(name, grads, slicers, shard_shapes):
    n = len(grads)

    def body(*refs):
        srcs, outs = refs[:n], refs[n:2 * n]
        send_sems, recv_sems = refs[2 * n:]
        x, y, c = _coords()
        sibling = (x, y, 1 - c)
        copies = []
        for w in range(n):
            for j in range(4):
                copies.append(pltpu.make_async_remote_copy(
                    src_ref=slicers[w](srcs[w], 2 * j + (1 - c)), dst_ref=outs[w].at[j],
                    send_sem=send_sems.at[w, j], recv_sem=recv_sems.at[w, j], device_id=sibling, device_id_type=MESH))
        for cp in copies:
            cp.start()
        for cp in copies:
            cp.wait_recv()
        for cp in copies:
            cp.wait_send()

    hbm = pl.BlockSpec(memory_space=pl.ANY)
    return pl.pallas_call(
        body, name=name,
        out_shape=[jax.ShapeDtypeStruct((4, *s), g.dtype) for s, g in zip(shard_shapes, grads)],
        in_specs=[hbm] * n, out_specs=[hbm] * n,
        scratch_shapes=[pltpu.SemaphoreType.DMA((n, 4)), pltpu.SemaphoreType.DMA((n, 4))],
        compiler_params=pltpu.CompilerParams(has_side_effects=True))(*grads)


def _pair_add(name, grad, recv, col_sharded, cidx):
    _, r, cc = recv.shape
    tr = _row_block(r, cc * 6, budget=8 << 20)
    nr = r // tr
    if col_sharded:
        g_spec = pl.BlockSpec((tr, cc), lambda j, i, s: (i, 2 * j + s[0]))
    else:
        g_spec = pl.BlockSpec((tr, cc), lambda j, i, s: ((2 * j + s[0]) * nr + i, 0))

    def body(s_ref, g_ref, r_ref, o_ref):
        del s_ref
        o_ref[...] = (g_ref[...].astype(F32) + r_ref[...].astype(F32)).astype(BF16)

    return pl.pallas_call(
        body, name=name,
        grid_spec=pltpu.PrefetchScalarGridSpec(
            num_scalar_prefetch=1, grid=(4, nr),
            in_specs=[g_spec, pl.BlockSpec((None, tr, cc), lambda j, i, s: (j, i, 0))],
            out_specs=pl.BlockSpec((None, tr, cc), lambda j, i, s: (j, i, 0))),
        out_shape=jax.ShapeDtypeStruct(recv.shape, BF16),
        compiler_params=_cparams(("parallel", "parallel"), tr * cc * 12))(cidx, grad, recv)


def _all_reduce_small(vec):
    r = vec.shape[0]

    def body(v_ref, o_ref, gath, send_sems, recv_sems):
        x, y, c = _coords()
        me = 4 * x + 2 * y + c
        copies = []
        for rel in range(1, N_DEV):
            peer = (x ^ (rel >> 2), y ^ ((rel >> 1) & 1), c ^ (rel & 1))
            copies.append(pltpu.make_async_remote_copy(
                src_ref=v_ref, dst_ref=gath.at[me], send_sem=send_sems.at[rel - 1], recv_sem=recv_sems.at[rel - 1],
                device_id=peer, device_id_type=MESH))
        for cp in copies:
            cp.start()
        gath[me] = v_ref[...]
        for rel in range(1, N_DEV):
            src = 4 * (x ^ (rel >> 2)) + 2 * (y ^ ((rel >> 1) & 1)) + (c ^ (rel & 1))
            pltpu.make_async_remote_copy(
                src_ref=v_ref, dst_ref=gath.at[src], send_sem=send_sems.at[rel - 1], recv_sem=recv_sems.at[rel - 1],
                device_id=(x, y, c), device_id_type=MESH).wait_recv()
        for cp in copies:
            cp.wait_send()
        acc = gath[0]
        for d in range(1, N_DEV):
            acc = acc + gath[d]
        o_ref[...] = acc

    vm = pl.BlockSpec(memory_space=pltpu.VMEM)
    return pl.pallas_call(
        body, name="all_reduce_small", out_shape=jax.ShapeDtypeStruct(vec.shape, F32), in_specs=[vm], out_specs=vm,
        scratch_shapes=[pltpu.VMEM((N_DEV, r, LANES), F32), pltpu.SemaphoreType.DMA((N_DEV - 1,)),
                        pltpu.SemaphoreType.DMA((N_DEV - 1,))],
        compiler_params=pltpu.CompilerParams(has_side_effects=True, vmem_limit_bytes=int(min(VMEM_CAP, 32 * r * LANES * 4 + (16 << 20)))))(vec)


def _adamw_math(w, g, m, v):
    m = ADAM_B1 * m + (1.0 - ADAM_B1) * g
    v = ADAM_B2 * v + (1.0 - ADAM_B2) * (g * g)
    m_hat = m / (1.0 - ADAM_B1 ** ADAM_STEP)
    v_hat = v / (1.0 - ADAM_B2 ** ADAM_STEP)
    delta = -ADAM_LR * (m_hat / (jnp.sqrt(v_hat) + ADAM_EPS) + ADAM_WD * w)
    return delta, m, v


def _adamw_big(name, pairsum, recv, w, m, v, chip_idx):
    r, cc = w.shape
    tr = _row_block(r, cc * 36, budget=16 << 20)

    def body(s_ref, p_ref, r_ref, w_ref, m_ref, v_ref, g_out, d_out, m_out, v_out):
        del s_ref
        g = p_ref[...].astype(F32)
        for k in range(3):
            g = g + r_ref[k].astype(F32)
        delta, m2, v2 = _adamw_math(w_ref[...], g, m_ref[...], v_ref[...])
        g_out[...] = g
        d_out[...] = delta
        m_out[...] = m2
        v_out[...] = v2

    tile = pl.BlockSpec((tr, cc), lambda i, s: (i, 0))
    sds = jax.ShapeDtypeStruct((r, cc), F32)
    return pl.pallas_call(
        body, name=name,
        grid_spec=pltpu.PrefetchScalarGridSpec(
            num_scalar_prefetch=1, grid=(r // tr,),
            in_specs=[pl.BlockSpec((None, tr, cc), lambda i, s: (s[0], i, 0)), pl.BlockSpec((3, tr, cc), lambda i, s: (0, i, 0)),
                      tile, tile, tile],
            out_specs=[tile, tile, tile, tile]),
        out_shape=[sds, sds, sds, sds],
        compiler_params=_cparams(("parallel",), tr * cc * 72))(chip_idx, pairsum, recv, w, m, v)


def _adamw_small(g, w, m, v):
    r = g.shape[0]

    def body(g_ref, w_ref, m_ref, v_ref, d_out, m_out, v_out):
        delta, m2, v2 = _adamw_math(w_ref[...], g_ref[...], m_ref[...], v_ref[...])
        d_out[...] = delta
        m_out[...] = m2
        v_out[...] = v2

    vm = pl.BlockSpec(memory_space=pltpu.VMEM)
    sds = jax.ShapeDtypeStruct((r, LANES), F32)
    return pl.pallas_call(body, name="adamw_small", out_shape=[sds, sds, sds], in_specs=[vm] * 4, out_specs=[vm] * 3,
                          compiler_params=_cparams(None, r * LANES * 28))(g, w, m, v)


def _pack(parts, rows):
    flat = jnp.concatenate([q.reshape(-1).astype(F32) for q in parts])
    return jnp.pad(flat, (0, rows * LANES - flat.shape[0])).reshape(rows, LANES)


def _unpack(packed, shapes):
    flat = packed.reshape(-1)
    out, off = [], 0
    for s in shapes:
        n = int(np.prod(s))
        out.append(flat[off:off + n].reshape(s))
        off += n
    return out


def kernel(x, p, norm_in_g, w_in, rel_table, w_dw, b_dw, conv_ln_g, conv_ln_b, w_pw, b_pw, attn_out_g, conv_out_g, w_out, ple_norm_g, w_ple_gate, b_ple_gate, w_ple, final_g, loss_target, m_norm_in_g, m_w_in, m_rel_table, m_w_dw, m_b_dw, m_conv_ln_g, m_conv_ln_b, m_w_pw, m_b_pw, m_attn_out_g, m_conv_out_g, m_w_out, m_ple_norm_g, m_w_ple_gate, m_b_ple_gate, m_w_ple, m_final_g, v_norm_in_g, v_w_in, v_rel_table, v_w_dw, v_b_dw, v_conv_ln_g, v_conv_ln_b, v_w_pw, v_b_pw, v_attn_out_g, v_conv_out_g, v_w_out, v_ple_norm_g, v_w_ple_gate, v_b_ple_gate, v_w_ple, v_final_g):
    t, d = x.shape[1], x.shape[2]
    a = d // 2
    ncol = 7 * a
    ns = ncol // N_DEV
    pd = p.shape[-1]
    heads = a // HEAD_DIM
    assert x.shape[0] == 1 and t % QB == 0 and ns % LANES == 0 and w_in.shape == (1, d, ns)
    cx, cy, cc_ = _coords()
    me = 4 * cx + 2 * cy + cc_
    core_idx = jnp.reshape(cc_, (1,)).astype(jnp.int32)
    chip_idx = jnp.reshape(2 * cx + cy, (1,)).astype(jnp.int32)

    x2, tgt, p2 = x[0], loss_target[0], p[0, 0]
    final_g2 = final_g.reshape(1, d)

    w_dw_pad = jnp.pad(w_dw[0], ((0, 32 - CONV_K), (0, 0)))
    (wf_in,) = _all_gather([_cast_bf16(w_in[0], "cast_w_in")], [_col_slice(ns)], [(d, ncol)])
    shards = [_cast_bf16(w_out[0], "cast_w_out"), _cast_bf16(w_ple_gate[0], "cast_w_gate"),
              _cast_bf16(w_pw[0], "cast_w_pw"), _cast_bf16(w_ple[0], "cast_w_ple"), w_dw_pad]
    slicers = [_row_slice(d // N_DEV), _row_slice(d // N_DEV), _row_slice(a // N_DEV), _col_slice(d // N_DEV), _col_slice(a // N_DEV)]
    full_shapes = [(d, d), (d, d), (a, a), (pd, d), (32, a)]

    xn = _rms_fwd(x2, norm_in_g, "rms_in_fwd")
    proj7, fulls = _proj_fwd(xn, wf_in, a, comm=_gather_ici_comm(shards, slicers, full_shapes))
    tvec = rel_table[0][:, _rel_index()].reshape(heads, 1, KB)
    bias = _bias_build(tvec)
    o, (wf_out, wf_gate, wf_pw, wf_ple, wf_dw) = _attn_fwd(proj7, bias, comm=_gather_d2d_comm(fulls, slicers))
    u1 = _conv_fwd(proj7, wf_dw, b_dw)
    u3 = _ln_silu_fwd(u1, conv_ln_g, conv_ln_b)
    yc = _mm_nn("pw_fwd", u3, wf_pw, F32, bias=b_pw)
    gains = jnp.stack([attn_out_g, conv_out_g])
    ycat = _post_fwd(o, yc, proj7, gains)
    h1 = _mm_nn("out_fwd", ycat, wf_out, F32, resid=x2)
    hn = _rms_fwd(h1, ple_norm_g, "rms_ple_fwd")
    h2, gate, pe = _gate_fwd(hn, wf_gate, b_ple_gate, h1, p2, wf_ple)

    dh2, dpe, dgl, loss_v, d_final_g, d_b_gate = _final(h2, tgt, gate, pe, final_g2)
    dhn = _mm_nt("gate_bwd_x", dgl, wf_gate, F32)
    dw_gate = _mm_tn("gate_bwd_w", hn, dgl, BF16)
    dw_ple = _mm_tn("ple_bwd_w", p2, dpe, BF16)
    dh1, dh1b, d_ple_norm_g = _rms_bwd(dhn, h1, ple_norm_g, dh2, "rms_ple_bwd", True)
    dycat = _mm_nt("out_bwd_x", dh1b, wf_out, F32)
    dw_out = _mm_tn("out_bwd_w", ycat, dh1b, BF16)
    dproj7, dxin, d_gains, cs_dxin = _post_bwd(dycat, o, yc, proj7, gains)
    dproj7, ds = _attn_bwd(dproj7, proj7, dxin, bias)
    dyc = dxin[1]
    du3 = _mm_nt("pw_bwd_x", dyc, wf_pw, F32)
    dw_pw = _mm_tn("pw_bwd_w", u3, dyc, BF16)
    du1, d_ln_g, d_ln_b, d_b_dw = _ln_silu_bwd(du3, u1, conv_ln_g, conv_ln_b)
    dproj7, d_w_dw = _conv_bwd(dproj7, du1, proj7, wf_dw)

    rest = [("w_out", dw_out, False, d // N_DEV, w_out, m_w_out, v_w_out),
            ("w_gate", dw_gate, False, d // N_DEV, w_ple_gate, m_w_ple_gate, v_w_ple_gate),
            ("w_pw", dw_pw, False, a // N_DEV, w_pw, m_w_pw, v_w_pw), ("w_ple", dw_ple, True, d // N_DEV, w_ple, m_w_ple, v_w_ple)]

    def pair_sums(tag, group):
        g_slicers = [(_col_slice if col else _row_slice)(sz) for _, _, col, sz, *_ in group]
        shard_shapes = [wt.shape[1:] for _, _, _, _, wt, _, _ in group]
        recv1 = _pair_exchange("grad_pair_exchange_" + tag, [g for _, g, *_ in group], g_slicers, shard_shapes)
        return [_pair_add("pair_add_" + nm, g, r1, col, core_idx) for (nm, g, col, *_), r1 in zip(group, recv1)]

    pairs_rest = pair_sums("rest", rest)
    dw_in, recv2_rest = _proj_bwd_w(xn, dproj7, comm=_chip_exchange_comm(pairs_rest))
    first = [("w_in", dw_in, True, ns, w_in, m_w_in, v_w_in)]
    pairs_in = pair_sums("w_in", first)
    dxn, recv2_in = _proj_bwd_x(dproj7, wf_in, comm=_chip_exchange_comm(pairs_in))
    grad_x, d_norm_in_g = _rms_bwd(dxn, x2, norm_in_g, dh1, "rms_in_bwd", False)

    dcol = _bias_grad(ds).reshape(heads, KB)
    onehot = jnp.asarray(_rel_index()[:, None] == np.arange(2 * MAX_REL + 1)[None, :], F32)
    d_rel = jnp.einsum("hj,jr->hr", dcol, onehot, precision=lax.Precision.HIGHEST)

    big_out = {}
    for (nm, _, _, _, wt, mt, vt), ps, r2 in zip(first + rest, pairs_in + pairs_rest, recv2_in + recv2_rest):
        g_, d_, m_, v_ = _adamw_big("adamw_" + nm, ps, r2, wt[0], mt[0], vt[0], chip_idx)
        big_out[nm] = tuple(q[None] for q in (g_, d_, m_, v_))

    small = [("norm_in_g", d_norm_in_g, norm_in_g, m_norm_in_g, v_norm_in_g),
             ("rel_table", d_rel, rel_table, m_rel_table, v_rel_table),
             ("b_dw", d_b_dw, b_dw, m_b_dw, v_b_dw),
             ("conv_ln_g", d_ln_g, conv_ln_g, m_conv_ln_g, v_conv_ln_g),
             ("conv_ln_b", d_ln_b, conv_ln_b, m_conv_ln_b, v_conv_ln_b),
             ("b_pw", cs_dxin[1], b_pw, m_b_pw, v_b_pw),
             ("attn_out_g", d_gains[0], attn_out_g, m_attn_out_g, v_attn_out_g),
             ("conv_out_g", d_gains[1], conv_out_g, m_conv_out_g, v_conv_out_g),
             ("ple_norm_g", d_ple_norm_g, ple_norm_g, m_ple_norm_g, v_ple_norm_g),
             ("b_ple_gate", d_b_gate, b_ple_gate, m_b_ple_gate, v_b_ple_gate),
             ("final_g", d_final_g, final_g, m_final_g, v_final_g)]
    g_parts = [g for _, g, *_ in small] + [d_w_dw[:CONV_K], loss_v[0, :1]]
    n_small = sum(int(np.prod(q.shape)) for q in g_parts)
    rows = -(-n_small // (8 * LANES)) * 8
    g_all = _all_reduce_small(_pack(g_parts, rows))
    shapes = [wt.shape for _, _, wt, _, _ in small] + [(CONV_K, a), (1,)]
    g_un = _unpack(g_all, shapes)
    loss = g_un[-1][0]
    g_w_dw = lax.dynamic_slice_in_dim(g_un[-2], me * (a // N_DEV), a // N_DEV, axis=1)[None]
    g_small = g_un[:len(small)] + [g_w_dw]
    rows2 = -(-(n_small - 1) // (8 * LANES)) * 8
    pk = lambda parts: _pack(parts, rows2)
    d_pk, m_pk, v_pk = _adamw_small(pk(g_small), pk([wt for _, _, wt, _, _ in small] + [w_dw]),
                                    pk([mt for _, _, _, mt, _ in small] + [m_w_dw]), pk([vt for _, _, _, _, vt in small] + [v_w_dw]))
    shapes2 = [wt.shape for _, _, wt, _, _ in small] + [w_dw.shape]
    names = [nm for nm, *_ in small] + ["w_dw"]
    small_out = {nm: (g, dd, mm, vv) for nm, g, dd, mm, vv in
                 zip(names, g_small, _unpack(d_pk, shapes2), _unpack(m_pk, shapes2), _unpack(v_pk, shapes2))}

    order = ["norm_in_g", "w_in", "rel_table", "w_dw", "b_dw", "conv_ln_g", "conv_ln_b", "w_pw", "b_pw", "attn_out_g",
             "conv_out_g", "w_out", "ple_norm_g", "w_ple_gate", "b_ple_gate", "w_ple", "final_g"]
    alias = {"w_ple_gate": "w_gate"}
    res = {nm: (big_out[alias.get(nm, nm)] if alias.get(nm, nm) in big_out else small_out[nm]) for nm in order}
    outs = [loss, grad_x[None]]
    for kind in range(4):
        outs += [res[nm][kind].reshape(w_shape) for nm, w_shape in zip(order, [
            norm_in_g.shape, w_in.shape, rel_table.shape, w_dw.shape, b_dw.shape, conv_ln_g.shape, conv_ln_b.shape, w_pw.shape,
            b_pw.shape, attn_out_g.shape, conv_out_g.shape, w_out.shape, ple_norm_g.shape, w_ple_gate.shape, b_ple_gate.shape,
            w_ple.shape, final_g.shape])]
    return tuple(outs)
```
